```python
import math
import jax, jax.numpy as jnp
from jax import lax
import numpy as np

D_MODEL = 2048
BATCH = 8
SEQ = 2048
DEPTH = 1
DEC_BATCH = 8
DEC_SEQ = 16
PAST_LEN = 4096

CHUNK = 64
WINDOW = 128
N_HEADS = 16
N_KV_HEADS = 2
HEAD_DIM = 64
GQA_GROUP = N_HEADS // N_KV_HEADS
D_ATTN = N_HEADS * HEAD_DIM
D_KV = N_KV_HEADS * HEAD_DIM
N_BUCKETS = 32
MAX_DISTANCE = 128
SGU_BLOCK = 128
SGU_GROUPS = 8
SGU_GROUP_DIM = 128
D_SGU = SGU_GROUPS * SGU_GROUP_DIM
PEER_HEADS = 8
PEER_N_KEYS = 128
PEER_N_EXPERTS = PEER_N_KEYS * PEER_N_KEYS
PEER_QUERY_DIM = 256
PEER_HALF = PEER_QUERY_DIM // 2
PEER_TOPK = 16
PEER_TOKEN_BLOCK = 128
D_IN = D_ATTN + 2 * D_KV + 2 * D_SGU + 2 * D_MODEL
SPLIT_POINTS = tuple(int(s) for s in np.cumsum([D_ATTN, D_KV, D_KV, D_SGU, D_SGU, D_MODEL]))
EPS = 1e-6
NEG_INF = -1e30

kernel_name = 'hybrid_swa_sgu_peer_stream_step'


def rmsnorm(x, g):
    xf = x.astype(jnp.float32)
    r = lax.rsqrt(jnp.mean(xf * xf, axis=-1, keepdims=True) + EPS)
    return (xf * r * g.astype(jnp.float32)).astype(x.dtype)


def t5_bucket(rel):
    half = N_BUCKETS // 2
    max_exact = half // 2
    ret = jnp.where(rel > 0, half, 0)
    n = jnp.abs(rel)
    nf = jnp.maximum(n, 1).astype(jnp.float32)
    large = max_exact + (jnp.log(nf / max_exact) / math.log(MAX_DISTANCE / max_exact)
                         * (half - max_exact)).astype(jnp.int32)
    large = jnp.minimum(large, half - 1)
    return ret + jnp.where(n < max_exact, n, large)


def rel_bias(table, n_q, n_k):
    rel = jnp.arange(n_k, dtype=jnp.int32)[None, :] - WINDOW - jnp.arange(n_q, dtype=jnp.int32)[:, None]
    b = table[t5_bucket(rel)].astype(jnp.float32)
    return b.transpose(2, 0, 1).reshape(N_KV_HEADS, GQA_GROUP, n_q, n_k)


def attend(q, k, v, bias, mask, sinks):
    s = jnp.einsum('...qkgd,...skd->...kgqs', q, k, preferred_element_type=jnp.float32)
    s = s * (HEAD_DIM ** -0.5) + bias
    if mask is not None:
        s = jnp.where(mask, s, NEG_INF)
    sink = jnp.broadcast_to(sinks.astype(jnp.float32).reshape(N_KV_HEADS, GQA_GROUP, 1, 1),
                            s.shape[:-1] + (1,))
    p = jax.nn.softmax(jnp.concatenate([s, sink], axis=-1), axis=-1)[..., :-1]
    return jnp.einsum('...kgqs,...skd->...qkgd', p.astype(v.dtype), v)


def swa_prompt(q, k, v, bias, sinks):
    B, S = q.shape[:2]
    n_c = S // CHUNK
    n_back = WINDOW // CHUNK
    qb = q.reshape(B, n_c, CHUNK, N_KV_HEADS, GQA_GROUP, HEAD_DIM)
    pad = ((0, 0), (WINDOW, 0), (0, 0), (0, 0))
    kp = jnp.pad(k, pad).reshape(B, n_c + n_back, CHUNK, N_KV_HEADS, HEAD_DIM)
    vp = jnp.pad(v, pad).reshape(B, n_c + n_back, CHUNK, N_KV_HEADS, HEAD_DIM)
    kb = jnp.concatenate([kp[:, i:i + n_c] for i in range(n_back + 1)], axis=2)
    vb = jnp.concatenate([vp[:, i:i + n_c] for i in range(n_back + 1)], axis=2)
    key_pos = (jnp.arange(n_c)[:, None] * CHUNK - WINDOW + jnp.arange(WINDOW + CHUNK)[None, :])
    mask = (key_pos >= 0)[None, :, None, None, None, :]
    o = attend(qb, kb, vb, bias, mask, sinks)
    return o.reshape(B, S, D_ATTN)


def sgu(u, vn, w_s_masked, b_s):
    L = u.shape[2]
    mixed = jnp.einsum('gts,bnsgc->bntgc', w_s_masked[:, :L, :L], vn)
    return u * (mixed + b_s[:, :L].T[:, :, None].astype(mixed.dtype))


def split_in(z, sgu_g):
    lead = z.shape[:-1]
    q, k, v, u, vs, ga, gb = jnp.split(z, SPLIT_POINTS, axis=-1)
    q = q.reshape(*lead, N_KV_HEADS, GQA_GROUP, HEAD_DIM)
    k = k.reshape(*lead, N_KV_HEADS, HEAD_DIM)
    v = v.reshape(*lead, N_KV_HEADS, HEAD_DIM)
    u = jax.nn.gelu(u)
    vs = rmsnorm(jax.nn.gelu(vs), sgu_g)
    return q, k, v, u, vs, ga, gb


def merge(o_attn, s_sgu, ga, gb, w_pa, w_pb, w_out):
    h = jax.nn.sigmoid(ga) * (o_attn @ w_pa) + jax.nn.sigmoid(gb) * (s_sgu @ w_pb)
    return h @ w_out


def peer_tokens(xt, w_query, sub_keys, expert_u, expert_v):
    T = xt.shape[0]
    q = (xt @ w_query).reshape(T, PEER_HEADS, 2, PEER_HALF)
    s = jnp.einsum('thpd,hpnd->thpn', q, sub_keys, preferred_element_type=jnp.float32)
    sv, si = lax.top_k(s, PEER_TOPK)
    cand = sv[:, :, 0, :, None] + sv[:, :, 1, None, :]
    cand_idx = si[:, :, 0, :, None] * PEER_N_KEYS + si[:, :, 1, None, :]
    cv, ci = lax.top_k(cand.reshape(T, PEER_HEADS, PEER_TOPK * PEER_TOPK), PEER_TOPK)
    eidx = jnp.take_along_axis(cand_idx.reshape(T, PEER_HEADS, PEER_TOPK * PEER_TOPK), ci, axis=-1)
    g = jax.nn.softmax(cv, axis=-1)
    ue = expert_u[eidx]
    a = jax.nn.gelu(jnp.einsum('thkd,td->thk', ue, xt, preferred_element_type=jnp.float32))
    h = (g * a).astype(xt.dtype)
    ve = expert_v[eidx]
    return jnp.einsum('thk,thkd->td', h, ve)


def setup_inputs(seed: int = 0) -> dict:
    key = jax.random.key(seed)
    ks = jax.random.split(key, 20)
    nrm = lambda k, shape, scale: jax.random.normal(k, shape, jnp.float32) * scale
    return {
        'x_prompt': nrm(ks[0], (BATCH, SEQ, D_MODEL), 1.0),
        'x_sample': nrm(ks[1], (DEC_BATCH, DEC_SEQ, D_MODEL), 1.0),
        'cache_k_swa': nrm(ks[2], (DEPTH, DEC_BATCH, WINDOW, N_KV_HEADS, HEAD_DIM), 1.0),
        'cache_v_swa': nrm(ks[3], (DEPTH, DEC_BATCH, WINDOW, N_KV_HEADS, HEAD_DIM), 1.0),
        'norm_mix_g': 1.0 + nrm(ks[4], (DEPTH, D_MODEL), 0.02),
        'w_in': nrm(ks[5], (DEPTH, D_MODEL, D_IN), D_MODEL ** -0.5),
        'sgu_norm_g': 1.0 + nrm(ks[6], (DEPTH, D_SGU), 0.02),
        'sgu_w_s': nrm(ks[7], (DEPTH, SGU_GROUPS, SGU_BLOCK, SGU_BLOCK), 0.5 * SGU_BLOCK ** -0.5),
        'sgu_b_s': 1.0 + nrm(ks[8], (DEPTH, SGU_GROUPS, SGU_BLOCK), 0.01),
        'attn_sinks': nrm(ks[9], (DEPTH, N_HEADS), 0.5),
        'rel_bias_table': nrm(ks[10], (N_BUCKETS, N_HEADS), 0.1),
        'w_branch_attn': nrm(ks[11], (DEPTH, D_ATTN, D_MODEL), D_ATTN ** -0.5),
        'w_branch_sgu': nrm(ks[12], (DEPTH, D_SGU, D_MODEL), D_SGU ** -0.5),
        'w_out': nrm(ks[13], (DEPTH, D_MODEL, D_MODEL), D_MODEL ** -0.5),
        'norm_ffn_g': 1.0 + nrm(ks[14], (DEPTH, D_MODEL), 0.02),
        'peer_w_query': nrm(ks[15], (DEPTH, D_MODEL, PEER_HEADS * PEER_QUERY_DIM), D_MODEL ** -0.5),
        'peer_sub_keys': nrm(ks[16], (DEPTH, PEER_HEADS, 2, PEER_N_KEYS, PEER_HALF), PEER_HALF ** -0.5),
        'peer_expert_u': nrm(ks[17], (DEPTH, PEER_N_EXPERTS, D_MODEL), D_MODEL ** -0.5),
        'peer_expert_v': nrm(ks[18], (DEPTH, PEER_N_EXPERTS, D_MODEL), 0.1),
        'norm_final_g': 1.0 + nrm(ks[19], (D_MODEL,), 0.02),
    }


def reference(x_prompt, x_sample, cache_k_swa, cache_v_swa, norm_mix_g, w_in, sgu_norm_g,
              sgu_w_s, sgu_b_s, attn_sinks, rel_bias_table, w_branch_attn, w_branch_sgu,
              w_out, norm_ffn_g, peer_w_query, peer_sub_keys, peer_expert_u, peer_expert_v,
              norm_final_g):
    B, S = x_prompt.shape[:2]
    Bd, T = x_sample.shape[:2]
    tril = jnp.tril(jnp.ones((SGU_BLOCK, SGU_BLOCK), dtype=bool))
    bias_p = rel_bias(rel_bias_table, CHUNK, WINDOW + CHUNK)
    bias_s = rel_bias(rel_bias_table, T, WINDOW + T)

    xp, xs = x_prompt, x_sample
    nk_p, nv_p, nk_s, nv_s, nsgu_s = [], [], [], [], []
    for l in range(DEPTH):
        w_s_masked = jnp.where(tril, sgu_w_s[l], 0.0).astype(sgu_w_s.dtype)
        zp = rmsnorm(xp, norm_mix_g[l]) @ w_in[l]
        qp, kp, vp, up, vsp, gap, gbp = split_in(zp, sgu_norm_g[l])
        op = swa_prompt(qp, kp, vp, bias_p, attn_sinks[l])
        n_blk = S // SGU_BLOCK
        sp = sgu(up.reshape(B, n_blk, SGU_BLOCK, SGU_GROUPS, SGU_GROUP_DIM),
                 vsp.reshape(B, n_blk, SGU_BLOCK, SGU_GROUPS, SGU_GROUP_DIM),
                 w_s_masked, sgu_b_s[l]).reshape(B, S, D_SGU)
        xp = xp + merge(op, sp, gap, gbp, w_branch_attn[l], w_branch_sgu[l], w_out[l])
        zs = rmsnorm(xs, norm_mix_g[l]) @ w_in[l]
        qs, kss, vss, us, vs_sgu, gas, gbs = split_in(zs, sgu_norm_g[l])
        k_all = jnp.concatenate([cache_k_swa[l].astype(kss.dtype), kss], axis=1)
        v_all = jnp.concatenate([cache_v_swa[l].astype(vss.dtype), vss], axis=1)
        os_ = attend(qs, k_all, v_all, bias_s, None, attn_sinks[l]).reshape(Bd, T, D_ATTN)
        ss = sgu(us.reshape(Bd, 1, T, SGU_GROUPS, SGU_GROUP_DIM),
                 vs_sgu.reshape(Bd, 1, T, SGU_GROUPS, SGU_GROUP_DIM),
                 w_s_masked, sgu_b_s[l]).reshape(Bd, T, D_SGU)
        xs = xs + merge(os_, ss, gas, gbs, w_branch_attn[l], w_branch_sgu[l], w_out[l])
        nk_p.append(kp[:, S - WINDOW:])
        nv_p.append(vp[:, S - WINDOW:])
        nk_s.append(kss)
        nv_s.append(vss)
        nsgu_s.append(vs_sgu)
        peer = lambda t: peer_tokens(t, peer_w_query[l], peer_sub_keys[l], peer_expert_u[l], peer_expert_v[l])
        xnp = rmsnorm(xp, norm_ffn_g[l]).reshape(-1, PEER_TOKEN_BLOCK, D_MODEL)
        xp = xp + lax.map(peer, xnp).reshape(B, S, D_MODEL)
        xns = rmsnorm(xs, norm_ffn_g[l]).reshape(Bd * T, D_MODEL)
        xs = xs + peer(xns).reshape(Bd, T, D_MODEL)

    y_prompt = rmsnorm(xp, norm_final_g)
    y_sample = rmsnorm(xs, norm_final_g)
    new_k_swa_prompt = jnp.stack(nk_p)
    new_v_swa_prompt = jnp.stack(nv_p)
    new_k_swa_sample = jnp.stack(nk_s)
    new_v_swa_sample = jnp.stack(nv_s)
    new_sgu_v_sample = jnp.stack(nsgu_s)
    return (y_prompt, y_sample, new_k_swa_prompt, new_v_swa_prompt, new_k_swa_sample, new_v_swa_sample, new_sgu_v_sample)
```

```python
import functools
import math

import jax
import jax.numpy as jnp
import numpy as np
from jax import lax
from jax.experimental import pallas as pl
from jax.experimental.pallas import tpu as pltpu

F32 = jnp.float32
BF16 = jnp.bfloat16

CHUNK = 64
WINDOW = 128
N_HEADS = 16
N_KV_HEADS = 2
HEAD_DIM = 64
GQA_GROUP = N_HEADS // N_KV_HEADS
D_ATTN = N_HEADS * HEAD_DIM
D_KV = N_KV_HEADS * HEAD_DIM
N_BUCKETS = 32
MAX_DISTANCE = 128
SGU_BLOCK = 128
SGU_GROUPS = 8
SGU_GROUP_DIM = 128
D_SGU = SGU_GROUPS * SGU_GROUP_DIM
PEER_HEADS = 8
PEER_N_KEYS = 128
PEER_HALF = 128
PEER_TOPK = 16
EPS = 1e-6
NEG_INF = -1e30

LANES = 128
SUBLANES = 8
VMEM_LIMIT_BYTES = 56 * 1024 * 1024

_CANDS = tuple((k, l) for k in range(PEER_TOPK) for l in range(PEER_TOPK)
               if (k + 1) * (l + 1) <= PEER_TOPK)


def _cparams(*sem):
    return pltpu.CompilerParams(dimension_semantics=sem, vmem_limit_bytes=VMEM_LIMIT_BYTES)


def _rms(x, g):
    r = lax.rsqrt(jnp.mean(x * x, axis=-1, keepdims=True) + EPS)
    return x * r * g


def _norm_kernel(x_ref, g_ref, o_ref):
    o_ref[...] = _rms(x_ref[...], g_ref[...]).astype(o_ref.dtype)


def _norm_cast(x, g, tm):
    t, d = x.shape
    return pl.pallas_call(
        _norm_kernel,
        grid=(t // tm,),
        in_specs=[pl.BlockSpec((tm, d), lambda i: (i, 0)),
                  pl.BlockSpec((1, d), lambda i: (0, 0))],
        out_specs=pl.BlockSpec((tm, d), lambda i: (i, 0)),
        out_shape=jax.ShapeDtypeStruct((t, d), BF16),
        compiler_params=_cparams("parallel"),
        name="rmsnorm_cast",
    )(x, g.reshape(1, d))


def _mm_kernel(x_ref, w_ref, *rest, epilogue):
    acc = jnp.dot(x_ref[...], w_ref[...], preferred_element_type=F32)
    if epilogue == "scale_q":
        (o_ref,) = rest
        o_ref[...] = (acc * (HEAD_DIM ** -0.5)).astype(o_ref.dtype)
    elif epilogue == "none":
        (o_ref,) = rest
        o_ref[...] = acc.astype(o_ref.dtype)
    elif epilogue == "gelu":
        (o_ref,) = rest
        o_ref[...] = jax.nn.gelu(acc).astype(o_ref.dtype)
    elif epilogue == "gelu_norm":
        g_ref, o_ref = rest
        o_ref[...] = _rms(jax.nn.gelu(acc), g_ref[...]).astype(o_ref.dtype)
    elif epilogue == "sigmoid":
        (o_ref,) = rest
        o_ref[...] = jax.nn.sigmoid(acc).astype(o_ref.dtype)
    else:
        raise ValueError(epilogue)


def _mm(x, w, epilogue, out_dtype, tm, tn, g=None, name="mm"):
    t, k = x.shape
    n = w.shape[1]
    in_specs = [pl.BlockSpec((tm, k), lambda i, j: (i, 0)),
                pl.BlockSpec((k, tn), lambda i, j: (0, j))]
    args = [x, w]
    if g is not None:
        in_specs.append(pl.BlockSpec((1, tn), lambda i, j: (0, j)))
        args.append(g.reshape(1, n))
    return pl.pallas_call(
        functools.partial(_mm_kernel, epilogue=epilogue),
        grid=(t // tm, n // tn),
        in_specs=in_specs,
        out_specs=pl.BlockSpec((tm, tn), lambda i, j: (i, j)),
        out_shape=jax.ShapeDtypeStruct((t, n), out_dtype),
        compiler_params=_cparams("parallel", "parallel"),
        name=name,
    )(*args)


def _bias_kernel(table_ref, bucket_ref, o_ref):
    bucket = bucket_ref[...]
    for h in range(N_HEADS):
        acc = jnp.zeros(bucket.shape, F32)
        for b in range(N_BUCKETS):
            acc = jnp.where(bucket == b, table_ref[b, h], acc)
        o_ref[h] = acc


def _t5_bucket(rel):
    half = N_BUCKETS // 2
    max_exact = half // 2
    ret = jnp.where(rel > 0, half, 0)
    n = jnp.abs(rel)
    nf = jnp.maximum(n, 1).astype(jnp.float32)
    large = max_exact + (jnp.log(nf / max_exact) / math.log(MAX_DISTANCE / max_exact)
                         * (half - max_exact)).astype(jnp.int32)
    large = jnp.minimum(large, half - 1)
    return ret + jnp.where(n < max_exact, n, large)


def _rel_bias(table, n_q, n_k):
    rel = (jnp.arange(n_k, dtype=jnp.int32)[None, :] - WINDOW
           - jnp.arange(n_q, dtype=jnp.int32)[:, None])
    bucket = _t5_bucket(rel).astype(jnp.int32)
    return pl.pallas_call(
        _bias_kernel,
        in_specs=[pl.BlockSpec(memory_space=pltpu.SMEM),
                  pl.BlockSpec(memory_space=pltpu.VMEM)],
        out_specs=pl.BlockSpec(memory_space=pltpu.VMEM),
        out_shape=jax.ShapeDtypeStruct((N_HEADS, n_q, n_k), F32),
        name="rel_bias",
    )(table.astype(F32), bucket)


def _attn_kernel(sinks_ref, q_ref, k_ref, v_ref, bias_ref, o_ref, *, n_q, n_k, n_pad):
    c = pl.program_id(1)
    start = pl.multiple_of(c * n_q, n_q)
    kb = k_ref[0, pl.ds(start, n_k), :]
    vb = v_ref[0, pl.ds(start, n_k), :]
    key_row = start + lax.broadcasted_iota(jnp.int32, (1, n_k), 1)
    valid = key_row >= n_pad
    for h in range(N_HEADS):
        kvh = h // GQA_GROUP
        qh = q_ref[0, :, h * HEAD_DIM:(h + 1) * HEAD_DIM]
        kh = kb[:, kvh * HEAD_DIM:(kvh + 1) * HEAD_DIM]
        vh = vb[:, kvh * HEAD_DIM:(kvh + 1) * HEAD_DIM]
        s = lax.dot_general(qh, kh, (((1,), (1,)), ((), ())), preferred_element_type=F32)
        s = s + bias_ref[h]
        s = jnp.where(valid, s, NEG_INF)
        sink = sinks_ref[h]
        m = jnp.maximum(jnp.max(s, axis=-1, keepdims=True), sink)
        p = jnp.exp(s - m)
        denom = jnp.sum(p, axis=-1, keepdims=True) + jnp.exp(sink - m)
        o = jnp.dot(p.astype(BF16), vh, preferred_element_type=F32)
        o_ref[0, :, h * HEAD_DIM:(h + 1) * HEAD_DIM] = (o / denom).astype(o_ref.dtype)


def _attention(q, k, v, bias, sinks, n_q, n_k, n_pad):
    nb, sq, _ = q.shape
    rows = k.shape[1]
    return pl.pallas_call(
        functools.partial(_attn_kernel, n_q=n_q, n_k=n_k, n_pad=n_pad),
        grid=(nb, sq // n_q),
        in_specs=[pl.BlockSpec(memory_space=pltpu.SMEM),
                  pl.BlockSpec((1, n_q, D_ATTN), lambda b, c: (b, c, 0)),
                  pl.BlockSpec((1, rows, D_KV), lambda b, c: (b, 0, 0)),
                  pl.BlockSpec((1, rows, D_KV), lambda b, c: (b, 0, 0)),
                  pl.BlockSpec((N_HEADS, n_q, n_k), lambda b, c: (0, 0, 0))],
        out_specs=pl.BlockSpec((1, n_q, D_ATTN), lambda b, c: (b, c, 0)),
        out_shape=jax.ShapeDtypeStruct((nb, sq, D_ATTN), BF16),
        compiler_params=_cparams("parallel", "arbitrary"),
        name="swa_attention",
    )(sinks.astype(F32), q, k, v, bias)


def _sgu_kernel(u_ref, v_ref, w_ref, b_ref, o_ref, *, rows):
    r = lax.broadcasted_iota(jnp.int32, (rows, rows), 0)
    c = lax.broadcasted_iota(jnp.int32, (rows, rows), 1)
    tril = c <= r
    for g in range(SGU_GROUPS):
        sl = slice(g * SGU_GROUP_DIM, (g + 1) * SGU_GROUP_DIM)
        w = jnp.where(tril, w_ref[g], 0.0).astype(BF16)
        mixed = jnp.dot(w, v_ref[0, :, sl].astype(BF16), preferred_element_type=F32)
        o_ref[0, :, sl] = (u_ref[0, :, sl].astype(F32) * (mixed + b_ref[:, sl])).astype(o_ref.dtype)


def _sgu(u, vn, w_s, b_s, rows):
    nblk = u.shape[0]
    w = w_s[:, :rows, :rows]
    b_full = jnp.repeat(b_s[:, :rows].T, SGU_GROUP_DIM, axis=1).astype(F32)
    return pl.pallas_call(
        functools.partial(_sgu_kernel, rows=rows),
        grid=(nblk,),
        in_specs=[pl.BlockSpec((1, rows, D_SGU), lambda i: (i, 0, 0)),
                  pl.BlockSpec((1, rows, D_SGU), lambda i: (i, 0, 0)),
                  pl.BlockSpec((SGU_GROUPS, rows, rows), lambda i: (0, 0, 0)),
                  pl.BlockSpec((rows, D_SGU), lambda i: (0, 0))],
        out_specs=pl.BlockSpec((1, rows, D_SGU), lambda i: (i, 0, 0)),
        out_shape=jax.ShapeDtypeStruct((nblk, rows, D_SGU), BF16),
        compiler_params=_cparams("parallel"),
        name="sgu",
    )(u, vn, w, b_full)


def _merge_kernel(o_ref, s_ref, gate_ref, x_ref, wpa_ref, wpb_ref, wout_ref, gn_ref,
                  xo_ref, xn_ref):
    d = x_ref.shape[-1]
    ta = jnp.dot(o_ref[...], wpa_ref[...], preferred_element_type=F32)
    tb = jnp.dot(s_ref[...], wpb_ref[...], preferred_element_type=F32)
    h = gate_ref[:, :d].astype(F32) * ta + gate_ref[:, d:].astype(F32) * tb
    y = x_ref[...] + jnp.dot(h.astype(BF16), wout_ref[...], preferred_element_type=F32)
    xo_ref[...] = y
    xn_ref[...] = _rms(y, gn_ref[...]).astype(xn_ref.dtype)


def _merge(o, s, gates, x, w_pa, w_pb, w_out, g_ffn, tm):
    t, d = x.shape
    once = pl.Buffered(1)
    return pl.pallas_call(
        _merge_kernel,
        grid=(t // tm,),
        in_specs=[pl.BlockSpec((tm, D_ATTN), lambda i: (i, 0)),
                  pl.BlockSpec((tm, D_SGU), lambda i: (i, 0)),
                  pl.BlockSpec((tm, 2 * d), lambda i: (i, 0)),
                  pl.BlockSpec((tm, d), lambda i: (i, 0)),
                  pl.BlockSpec((D_ATTN, d), lambda i: (0, 0), pipeline_mode=once),
                  pl.BlockSpec((D_SGU, d), lambda i: (0, 0), pipeline_mode=once),
                  pl.BlockSpec((d, d), lambda i: (0, 0), pipeline_mode=once),
                  pl.BlockSpec((1, d), lambda i: (0, 0))],
        out_specs=[pl.BlockSpec((tm, d), lambda i: (i, 0)),
                   pl.BlockSpec((tm, d), lambda i: (i, 0))],
        out_shape=[jax.ShapeDtypeStruct((t, d), F32),
                   jax.ShapeDtypeStruct((t, d), BF16)],
        compiler_params=_cparams("parallel"),
        name="merge",
    )(o, s, gates, x, w_pa, w_pb, w_out, g_ffn.reshape(1, d))


def _extract_topk(work_ref, rank_ref, top_ref):
    n_keys = work_ref.shape[0]

    def body(r, carry):
        m = work_ref[0]
        for n in range(1, n_keys):
            m = jnp.maximum(m, work_ref[n])
        idx = jnp.full(m.shape, float(n_keys), F32)
        for n in range(n_keys):
            idx = jnp.minimum(idx, jnp.where(work_ref[n] == m, float(n), float(n_keys)))
        rf = r.astype(F32)
        for n in range(n_keys):
            hit = idx == float(n)
            work_ref[n] = jnp.where(hit, -jnp.inf, work_ref[n])
            rank_ref[n] = jnp.where(hit, rf, rank_ref[n])
        top_ref[r] = m
        return carry

    lax.fori_loop(0, PEER_TOPK, body, 0)


def _peer_topk_kernel(q_ref, keys_ref, rankb_ref, e1_ref, nsel_ref, coef_ref,
                      s_ref, work_ref, rank0_ref, rank1_ref, top0_ref, top1_ref,
                      cand_ref, cnt_ref):
    nk = PEER_N_KEYS
    q = q_ref[...]
    for hp in range(2 * PEER_HEADS):
        s_ref[hp * nk:(hp + 1) * nk, :] = lax.dot_general(
            keys_ref[hp], q[:, hp * PEER_HALF:(hp + 1) * PEER_HALF],
            (((1,), (1,)), ((), ())), preferred_element_type=F32)

    def scores(p, n):
        return s_ref[pl.ds(p * nk + n, PEER_HEADS, stride=2 * nk), :]

    for p, rank_ref, top_ref in ((0, rank0_ref, top0_ref), (1, rank1_ref, top1_ref)):
        for n in range(nk):
            work_ref[n] = scores(p, n)
            rank_ref[n] = jnp.full(work_ref.shape[1:], float(PEER_TOPK), F32)
        _extract_topk(work_ref, rank_ref, top_ref)

    for c, (k, l) in enumerate(_CANDS):
        cand_ref[c] = top0_ref[k] + top1_ref[l]
    for k in range(PEER_TOPK):
        cnt_ref[k] = jnp.zeros(cnt_ref.shape[1:], F32)
    cv0 = top0_ref[0] + top1_ref[0]
    big = float(PEER_TOPK * PEER_TOPK)

    def body(r, z):
        m = cand_ref[0]
        for c in range(1, len(_CANDS)):
            m = jnp.maximum(m, cand_ref[c])
        idx = jnp.full(m.shape, big, F32)
        for c, (k, l) in enumerate(_CANDS):
            idx = jnp.minimum(idx, jnp.where(cand_ref[c] == m, float(k * PEER_TOPK + l), big))
        for k in range(PEER_TOPK):
            cnt = cnt_ref[k]
            for c, (kc, l) in enumerate(_CANDS):
                if kc != k:
                    continue
                hit = idx == float(k * PEER_TOPK + l)
                cand_ref[c] = jnp.where(hit, -jnp.inf, cand_ref[c])
                cnt = cnt + jnp.where(hit, 1.0, 0.0)
            cnt_ref[k] = cnt
        return z + jnp.exp(m - cv0)

    z = lax.fori_loop(0, PEER_TOPK, body, jnp.zeros(cv0.shape, F32))
    zinv = 1.0 / z
    a0 = top0_ref[0]
    b0 = top1_ref[0]
    for n in range(nk):
        rows = pl.ds(n, PEER_HEADS, stride=nk)
        ra = rank0_ref[n]
        nsel = jnp.zeros(ra.shape, F32)
        for k in range(PEER_TOPK):
            nsel = jnp.where(ra == float(k), cnt_ref[k], nsel)
        nsel_ref[rows, :] = nsel
        coef_ref[rows, :] = jnp.exp(scores(0, n) - a0) * zinv
        rankb_ref[rows, :] = rank1_ref[n]
        e1_ref[rows, :] = jnp.exp(scores(1, n) - b0)


def _peer_topk(q, keys, tb):
    t = q.shape[0]
    rows = PEER_HEADS * PEER_N_KEYS
    out = jax.ShapeDtypeStruct((rows, t), F32)
    ospec = pl.BlockSpec((rows, tb), lambda i: (0, i))
    hv = (PEER_HEADS, tb)
    return pl.pallas_call(
        _peer_topk_kernel,
        grid=(t // tb,),
        in_specs=[pl.BlockSpec((tb, q.shape[1]), lambda i: (i, 0)),
                  pl.BlockSpec(keys.shape, lambda i: (0, 0, 0))],
        out_specs=[ospec, ospec, ospec, ospec],
        out_shape=[out, out, out, out],
        scratch_shapes=[pltpu.VMEM((2 * rows, tb), F32),
                        pltpu.VMEM((PEER_N_KEYS,) + hv, F32),
                        pltpu.VMEM((PEER_N_KEYS,) + hv, F32),
                        pltpu.VMEM((PEER_N_KEYS,) + hv, F32),
                        pltpu.VMEM((PEER_TOPK,) + hv, F32),
                        pltpu.VMEM((PEER_TOPK,) + hv, F32),
                        pltpu.VMEM((len(_CANDS),) + hv, F32),
                        pltpu.VMEM((PEER_TOPK,) + hv, F32)],
        compiler_params=_cparams("parallel"),
        name="peer_topk",
    )(q, keys)


def _peer_expert_kernel(xt_ref, u_ref, vt_ref, rankb_ref, e1_ref, nsel_ref, coef_ref,
                        xres_ref, gf_ref, y_ref, acc_ref, h_ref, *, n_i, final_norm):
    et = pl.program_id(1)
    nk = PEER_N_KEYS

    @pl.when(et == 0)
    def _():
        acc_ref[...] = jnp.zeros_like(acc_ref)

    a_t = jnp.dot(u_ref[...], xt_ref[...], preferred_element_type=F32)
    for il in range(n_i):
        i = et * n_i + il
        gate = None
        for h in range(PEER_HEADS):
            hs = slice(h * nk, (h + 1) * nk)
            nsel_row = nsel_ref[pl.ds(h * nk + i, 1), :]
            coef_row = coef_ref[pl.ds(h * nk + i, 1), :]
            sel = jnp.where(rankb_ref[hs, :] < nsel_row, e1_ref[hs, :], 0.0) * coef_row
            gate = sel if gate is None else gate + sel
        rows = slice(il * nk, (il + 1) * nk)
        h_ref[rows, :] = (gate * jax.nn.gelu(a_t[rows, :])).astype(h_ref.dtype)
    acc_ref[...] += jnp.dot(vt_ref[...], h_ref[...], preferred_element_type=F32)

    @pl.when(et == pl.num_programs(1) - 1)
    def _():
        y = xres_ref[...] + acc_ref[...].T
        if final_norm:
            y = _rms(y, gf_ref[...])
        y_ref[...] = y


def _peer_experts(xn_t, u, v_t, rankb, e1, nsel, coef, xres, g_final, tb, e_tile, final_norm):
    d, t = xn_t.shape
    n_exp = u.shape[0]
    rows = PEER_HEADS * PEER_N_KEYS
    aux = pl.BlockSpec((rows, tb), lambda i, j: (0, i))
    return pl.pallas_call(
        functools.partial(_peer_expert_kernel, n_i=e_tile // PEER_N_KEYS, final_norm=final_norm),
        grid=(t // tb, n_exp // e_tile),
        in_specs=[pl.BlockSpec((d, tb), lambda i, j: (0, i)),
                  pl.BlockSpec((e_tile, d), lambda i, j: (j, 0)),
                  pl.BlockSpec((d, e_tile), lambda i, j: (0, j)),
                  aux, aux, aux, aux,
                  pl.BlockSpec((tb, d), lambda i, j: (i, 0)),
                  pl.BlockSpec((1, d), lambda i, j: (0, 0))],
        out_specs=pl.BlockSpec((tb, d), lambda i, j: (i, 0)),
        out_shape=jax.ShapeDtypeStruct((t, d), F32),
        scratch_shapes=[pltpu.VMEM((d, tb), F32),
                        pltpu.VMEM((e_tile, tb), BF16)],
        compiler_params=_cparams("parallel", "arbitrary"),
        name="peer_experts",
    )(xn_t, u, v_t, rankb, e1, nsel, coef, xres, g_final.reshape(1, d))


def _row_tile(t, pref):
    return pref if t % pref == 0 else t


def _in_projection(x2, g_mix, w_in, sgu_g):
    t = x2.shape[0]
    xn = _norm_cast(x2, g_mix, _row_tile(t, 512))
    tm = _row_tile(t, 1024)
    c0, c1, c2, c3, c4 = D_ATTN, D_ATTN + 2 * D_KV, D_ATTN + 2 * D_KV + D_SGU, \
        D_ATTN + 2 * D_KV + 2 * D_SGU, w_in.shape[1]
    q = _mm(xn, w_in[:, :c0], "scale_q", BF16, tm, 1024, name="proj_q")
    kv = _mm(xn, w_in[:, c0:c1], "none", F32, tm, 2 * D_KV, name="proj_kv")
    u = _mm(xn, w_in[:, c1:c2], "gelu", BF16, tm, 1024, name="proj_u")
    vs = _mm(xn, w_in[:, c2:c3], "gelu_norm", F32, _row_tile(t, 512), D_SGU, g=sgu_g, name="proj_vsgu")
    gates = _mm(xn, w_in[:, c3:c4], "sigmoid", BF16, tm, 1024, name="proj_gates")
    return q, kv, u, vs, gates


def _peer(xn, xres, w_query, keys, u, v_t, g_final, final_norm):
    t = xn.shape[0]
    q = _mm(xn, w_query, "none", BF16, _row_tile(t, 1024), 1024, name="peer_query")
    rankb, e1, nsel, coef = _peer_topk(q, keys, LANES)
    tb = _row_tile(t, 256)
    return _peer_experts(xn.T, u, v_t, rankb, e1, nsel, coef, xres, g_final, tb, 1024, final_norm)


def kernel(x_prompt, x_sample, cache_k_swa, cache_v_swa, norm_mix_g, w_in, sgu_norm_g, sgu_w_s,
           sgu_b_s, attn_sinks, rel_bias_table, w_branch_attn, w_branch_sgu, w_out, norm_ffn_g,
           peer_w_query, peer_sub_keys, peer_expert_u, peer_expert_v, norm_final_g):
    B, S, D = x_prompt.shape
    Bd, T, _ = x_sample.shape
    depth = w_in.shape[0]
    bias_p = _rel_bias(rel_bias_table, CHUNK, WINDOW + CHUNK)
    bias_s = _rel_bias(rel_bias_table, T, WINDOW + T)

    xp = x_prompt.reshape(B * S, D)
    xs = x_sample.reshape(Bd * T, D)
    nk_p, nv_p, nk_s, nv_s, nsgu_s = [], [], [], [], []
    for l in range(depth):
        last = l == depth - 1
        w_in_l = w_in[l].astype(BF16)
        w_pa = w_branch_attn[l].astype(BF16)
        w_pb = w_branch_sgu[l].astype(BF16)
        w_o = w_out[l].astype(BF16)
        w_q = peer_w_query[l].astype(BF16)
        keys = peer_sub_keys[l].astype(BF16).reshape(2 * PEER_HEADS, PEER_N_KEYS, PEER_HALF)
        e_u = peer_expert_u[l].astype(BF16)
        e_vt = peer_expert_v[l].astype(BF16).T
        g_out = norm_final_g if last else jnp.ones_like(norm_final_g)

        q, kv, u, vs, gates = _in_projection(xp, norm_mix_g[l], w_in_l, sgu_norm_g[l])
        kv3 = kv.reshape(B, S, 2 * D_KV)
        kpad = jnp.pad(kv3[..., :D_KV].astype(BF16), ((0, 0), (WINDOW, 0), (0, 0)))
        vpad = jnp.pad(kv3[..., D_KV:].astype(BF16), ((0, 0), (WINDOW, 0), (0, 0)))
        o = _attention(q.reshape(B, S, D_ATTN), kpad, vpad, bias_p, attn_sinks[l],
                       CHUNK, WINDOW + CHUNK, WINDOW)
        n_blk = S // SGU_BLOCK
        sg = _sgu(u.reshape(B * n_blk, SGU_BLOCK, D_SGU), vs.reshape(B * n_blk, SGU_BLOCK, D_SGU),
                  sgu_w_s[l], sgu_b_s[l], SGU_BLOCK)
        xp, xnp = _merge(o.reshape(B * S, D_ATTN), sg.reshape(B * S, D_SGU), gates, xp,
                         w_pa, w_pb, w_o, norm_ffn_g[l], _row_tile(B * S, 256))
        nk_p.append(kv3[:, S - WINDOW:, :D_KV].reshape(B, WINDOW, N_KV_HEADS, HEAD_DIM))
        nv_p.append(kv3[:, S - WINDOW:, D_KV:].reshape(B, WINDOW, N_KV_HEADS, HEAD_DIM))

        qs, kvs, us, vss, gates_s = _in_projection(xs, norm_mix_g[l], w_in_l, sgu_norm_g[l])
        kvs3 = kvs.reshape(Bd, T, 2 * D_KV)
        k_all = jnp.concatenate([cache_k_swa[l].reshape(Bd, WINDOW, D_KV).astype(BF16),
                                 kvs3[..., :D_KV].astype(BF16)], axis=1)
        v_all = jnp.concatenate([cache_v_swa[l].reshape(Bd, WINDOW, D_KV).astype(BF16),
                                 kvs3[..., D_KV:].astype(BF16)], axis=1)
        os_ = _attention(qs.reshape(Bd, T, D_ATTN), k_all, v_all, bias_s, attn_sinks[l],
                         T, WINDOW + T, 0)
        ss = _sgu(us.reshape(Bd, T, D_SGU), vss.reshape(Bd, T, D_SGU), sgu_w_s[l], sgu_b_s[l], T)
        xs, xns = _merge(os_.reshape(Bd * T, D_ATTN), ss.reshape(Bd * T, D_SGU), gates_s, xs,
                         w_pa, w_pb, w_o, norm_ffn_g[l], Bd * T)
        nk_s.append(kvs3[..., :D_KV].reshape(Bd, T, N_KV_HEADS, HEAD_DIM))
        nv_s.append(kvs3[..., D_KV:].reshape(Bd, T, N_KV_HEADS, HEAD_DIM))
        nsgu_s.append(vss.reshape(Bd, T, D_SGU))

        xp = _peer(xnp, xp, w_q, keys, e_u, e_vt, g_out, last)
        xs = _peer(xns, xs, w_q, keys, e_u, e_vt, g_out, last)

    return (xp.reshape(B, S, D), xs.reshape(Bd, T, D), jnp.stack(nk_p), jnp.stack(nv_p),
            jnp.stack(nk_s), jnp.stack(nv_s), jnp.stack(nsgu_s))
```

```python
import functools
import math

import jax
import jax.numpy as jnp
import numpy as np
from jax import lax
from jax.experimental import pallas as pl
from jax.experimental.pallas import tpu as pltpu

F32 = jnp.float32
BF16 = jnp.bfloat16

CHUNK = 64
WINDOW = 128
N_HEADS = 16
N_KV_HEADS = 2
HEAD_DIM = 64
GQA_GROUP = N_HEADS // N_KV_HEADS
D_ATTN = N_HEADS * HEAD_DIM
D_KV = N_KV_HEADS * HEAD_DIM
N_BUCKETS = 32
MAX_DISTANCE = 128
SGU_BLOCK = 128
SGU_GROUPS = 8
SGU_GROUP_DIM = 128
D_SGU = SGU_GROUPS * SGU_GROUP_DIM
PEER_HEADS = 8
PEER_N_KEYS = 128
PEER_HALF = 128
PEER_TOPK = 16
EPS = 1e-6
NEG_INF = -1e30

LANES = 128
SUBLANES = 8
VMEM_LIMIT_BYTES = 56 * 1024 * 1024

_CANDS = tuple((k, l) for k in range(PEER_TOPK) for l in range(PEER_TOPK)
               if (k + 1) * (l + 1) <= PEER_TOPK)


def _cparams(*sem):
    return pltpu.CompilerParams(dimension_semantics=sem, vmem_limit_bytes=VMEM_LIMIT_BYTES)


def _rms(x, g):
    r = lax.rsqrt(jnp.mean(x * x, axis=-1, keepdims=True) + EPS)
    return x * r * g


def _norm_kernel(x_ref, g_ref, o_ref):
    o_ref[...] = _rms(x_ref[...], g_ref[...]).astype(o_ref.dtype)


def _norm_cast(x, g, tm):
    t, d = x.shape
    return pl.pallas_call(
        _norm_kernel,
        grid=(t // tm,),
        in_specs=[pl.BlockSpec((tm, d), lambda i: (i, 0)),
                  pl.BlockSpec((1, d), lambda i: (0, 0))],
        out_specs=pl.BlockSpec((tm, d), lambda i: (i, 0)),
        out_shape=jax.ShapeDtypeStruct((t, d), BF16),
        compiler_params=_cparams("parallel"),
        name="rmsnorm_cast",
    )(x, g.reshape(1, d))


def _mm_kernel(x_ref, w_ref, *rest, epilogue):
    acc = jnp.dot(x_ref[...], w_ref[...], preferred_element_type=F32)
    if epilogue == "scale_q":
        (o_ref,) = rest
        o_ref[...] = (acc * (HEAD_DIM ** -0.5)).astype(o_ref.dtype)
    elif epilogue == "none":
        (o_ref,) = rest
        o_ref[...] = acc.astype(o_ref.dtype)
    elif epilogue == "gelu":
        (o_ref,) = rest
        o_ref[...] = jax.nn.gelu(acc).astype(o_ref.dtype)
    elif epilogue == "gelu_norm":
        g_ref, o_ref = rest
        o_ref[...] = _rms(jax.nn.gelu(acc), g_ref[...]).astype(o_ref.dtype)
    elif epilogue == "sigmoid":
        (o_ref,) = rest
        o_ref[...] = jax.nn.sigmoid(acc).astype(o_ref.dtype)
    else:
        raise ValueError(epilogue)


def _mm(x, w, epilogue, out_dtype, tm, tn, g=None, name="mm"):
    t, k = x.shape
    n = w.shape[1]
    in_specs = [pl.BlockSpec((tm, k), lambda i, j: (i, 0)),
                pl.BlockSpec((k, tn), lambda i, j: (0, j))]
    args = [x, w]
    if g is not None:
        in_specs.append(pl.BlockSpec((1, tn), lambda i, j: (0, j)))
        args.append(g.reshape(1, n))
    return pl.pallas_call(
        functools.partial(_mm_kernel, epilogue=epilogue),
        grid=(t // tm, n // tn),
        in_specs=in_specs,
        out_specs=pl.BlockSpec((tm, tn), lambda i, j: (i, j)),
        out_shape=jax.ShapeDtypeStruct((t, n), out_dtype),
        compiler_params=_cparams("parallel", "parallel"),
        name=name,
    )(*args)


def _bias_kernel(table_ref, bucket_ref, o_ref):
    bucket = bucket_ref[...]
    for h in range(N_HEADS):
        acc = jnp.zeros(bucket.shape, F32)
        for b in range(N_BUCKETS):
            acc = jnp.where(bucket == b, table_ref[b, h], acc)
        o_ref[h] = acc


def _t5_bucket(rel):
    half = N_BUCKETS // 2
    max_exact = half // 2
    ret = jnp.where(rel > 0, half, 0)
    n = jnp.abs(rel)
    nf = jnp.maximum(n, 1).astype(jnp.float32)
    large = max_exact + (jnp.log(nf / max_exact) / math.log(MAX_DISTANCE / max_exact)
                         * (half - max_exact)).astype(jnp.int32)
    large = jnp.minimum(large, half - 1)
    return ret + jnp.where(n < max_exact, n, large)


def _rel_bias(table, n_q, n_k):
    rel = (jnp.arange(n_k, dtype=jnp.int32)[None, :] - WINDOW
           - jnp.arange(n_q, dtype=jnp.int32)[:, None])
    bucket = _t5_bucket(rel).astype(jnp.int32)
    return pl.pallas_call(
        _bias_kernel,
        in_specs=[pl.BlockSpec(memory_space=pltpu.SMEM),
                  pl.BlockSpec(memory_space=pltpu.VMEM)],
        out_specs=pl.BlockSpec(memory_space=pltpu.VMEM),
        out_shape=jax.ShapeDtypeStruct((N_HEADS, n_q, n_k), F32),
        name="rel_bias",
    )(table.astype(F32), bucket)


def _attn_kernel(sinks_ref, q_ref, k_ref, v_ref, bias_ref, o_ref, *, n_q, n_k, n_pad):
    c = pl.program_id(1)
    start = pl.multiple_of(c * n_q, n_q)
    kb = k_ref[0, pl.ds(start, n_k), :]
    vb = v_ref[0, pl.ds(start, n_k), :]
    key_row = start + lax.broadcasted_iota(jnp.int32, (1, n_k), 1)
    valid = key_row >= n_pad
    for h in range(N_HEADS):
        kvh = h // GQA_GROUP
        qh = q_ref[0, :, h * HEAD_DIM:(h + 1) * HEAD_DIM]
        kh = kb[:, kvh * HEAD_DIM:(kvh + 1) * HEAD_DIM]
        vh = vb[:, kvh * HEAD_DIM:(kvh + 1) * HEAD_DIM]
        s = lax.dot_general(qh, kh, (((1,), (1,)), ((), ())), preferred_element_type=F32)
        s = s + bias_ref[h]
        s = jnp.where(valid, s, NEG_INF)
        sink = sinks_ref[h]
        m = jnp.maximum(jnp.max(s, axis=-1, keepdims=True), sink)
        p = jnp.exp(s - m)
        denom = jnp.sum(p, axis=-1, keepdims=True) + jnp.exp(sink - m)
        o = jnp.dot(p.astype(BF16), vh, preferred_element_type=F32)
        o_ref[0, :, h * HEAD_DIM:(h + 1) * HEAD_DIM] = (o / denom).astype(o_ref.dtype)


def _attention(q, k, v, bias, sinks, n_q, n_k, n_pad):
    nb, sq, _ = q.shape
    rows = k.shape[1]
    return pl.pallas_call(
        functools.partial(_attn_kernel, n_q=n_q, n_k=n_k, n_pad=n_pad),
        grid=(nb, sq // n_q),
        in_specs=[pl.BlockSpec(memory_space=pltpu.SMEM),
                  pl.BlockSpec((1, n_q, D_ATTN), lambda b, c: (b, c, 0)),
                  pl.BlockSpec((1, rows, D_KV), lambda b, c: (b, 0, 0)),
                  pl.BlockSpec((1, rows, D_KV), lambda b, c: (b, 0, 0)),
                  pl.BlockSpec((N_HEADS, n_q, n_k), lambda b, c: (0, 0, 0))],
        out_specs=pl.BlockSpec((1, n_q, D_ATTN), lambda b, c: (b, c, 0)),
        out_shape=jax.ShapeDtypeStruct((nb, sq, D_ATTN), BF16),
        compiler_params=_cparams("parallel", "arbitrary"),
        name="swa_attention",
    )(sinks.astype(F32), q, k, v, bias)


def _sgu_kernel(u_ref, v_ref, w_ref, b_ref, o_ref, *, rows):
    r = lax.broadcasted_iota(jnp.int32, (rows, rows), 0)
    c = lax.broadcasted_iota(jnp.int32, (rows, rows), 1)
    tril = c <= r
    for g in range(SGU_GROUPS):
        sl = slice(g * SGU_GROUP_DIM, (g + 1) * SGU_GROUP_DIM)
        w = jnp.where(tril, w_ref[g], 0.0).astype(BF16)
        mixed = jnp.dot(w, v_ref[0, :, sl].astype(BF16), preferred_element_type=F32)
        o_ref[0, :, sl] = (u_ref[0, :, sl].astype(F32) * (mixed + b_ref[:, sl])).astype(o_ref.dtype)


def _sgu(u, vn, w_s, b_s, rows):
    nblk = u.shape[0]
    w = w_s[:, :rows, :rows]
    b_full = jnp.repeat(b_s[:, :rows].T, SGU_GROUP_DIM, axis=1).astype(F32)
    return pl.pallas_call(
        functools.partial(_sgu_kernel, rows=rows),
        grid=(nblk,),
        in_specs=[pl.BlockSpec((1, rows, D_SGU), lambda i: (i, 0, 0)),
                  pl.BlockSpec((1, rows, D_SGU), lambda i: (i, 0, 0)),
                  pl.BlockSpec((SGU_GROUPS, rows, rows), lambda i: (0, 0, 0)),
                  pl.BlockSpec((rows, D_SGU), lambda i: (0, 0))],
        out_specs=pl.BlockSpec((1, rows, D_SGU), lambda i: (i, 0, 0)),
        out_shape=jax.ShapeDtypeStruct((nblk, rows, D_SGU), BF16),
        compiler_params=_cparams("parallel"),
        name="sgu",
    )(u, vn, w, b_full)


def _merge_kernel(o_ref, s_ref, gate_ref, x_ref, wpa_ref, wpb_ref, wout_ref, gn_ref,
                  xo_ref, xn_ref):
    d = x_ref.shape[-1]
    ta = jnp.dot(o_ref[...], wpa_ref[...], preferred_element_type=F32)
    tb = jnp.dot(s_ref[...], wpb_ref[...], preferred_element_type=F32)
    h = gate_ref[:, :d].astype(F32) * ta + gate_ref[:, d:].astype(F32) * tb
    y = x_ref[...] + jnp.dot(h.astype(BF16), wout_ref[...], preferred_element_type=F32)
    xo_ref[...] = y
    xn_ref[...] = _rms(y, gn_ref[...]).astype(xn_ref.dtype)


def _merge(o, s, gates, x, w_pa, w_pb, w_out, g_ffn, tm):
    t, d = x.shape
    once = pl.Buffered(1)
    return pl.pallas_call(
        _merge_kernel,
        grid=(t // tm,),
        in_specs=[pl.BlockSpec((tm, D_ATTN), lambda i: (i, 0)),
                  pl.BlockSpec((tm, D_SGU), lambda i: (i, 0)),
                  pl.BlockSpec((tm, 2 * d), lambda i: (i, 0)),
                  pl.BlockSpec((tm, d), lambda i: (i, 0)),
                  pl.BlockSpec((D_ATTN, d), lambda i: (0, 0), pipeline_mode=once),
                  pl.BlockSpec((D_SGU, d), lambda i: (0, 0), pipeline_mode=once),
                  pl.BlockSpec((d, d), lambda i: (0, 0), pipeline_mode=once),
                  pl.BlockSpec((1, d), lambda i: (0, 0))],
        out_specs=[pl.BlockSpec((tm, d), lambda i: (i, 0)),
                   pl.BlockSpec((tm, d), lambda i: (i, 0))],
        out_shape=[jax.ShapeDtypeStruct((t, d), F32),
                   jax.ShapeDtypeStruct((t, d), BF16)],
        compiler_params=_cparams("parallel"),
        name="merge",
    )(o, s, gates, x, w_pa, w_pb, w_out, g_ffn.reshape(1, d))


def _extract_topk(work_ref, rank_ref, top_ref):
    n_keys = work_ref.shape[0]

    def body(r, carry):
        m = work_ref[0]
        for n in range(1, n_keys):
            m = jnp.maximum(m, work_ref[n])
        idx = jnp.full(m.shape, float(n_keys), F32)
        for n in range(n_keys):
            idx = jnp.minimum(idx, jnp.where(work_ref[n] == m, float(n), float(n_keys)))
        rf = r.astype(F32)
        for n in range(n_keys):
            hit = idx == float(n)
            work_ref[n] = jnp.where(hit, -jnp.inf, work_ref[n])
            rank_ref[n] = jnp.where(hit, rf, rank_ref[n])
        top_ref[r] = m
        return carry

    lax.fori_loop(0, PEER_TOPK, body, 0)


def _peer_topk_kernel(q_ref, keys_ref, rankb_ref, e1_ref, nsel_ref, coef_ref,
                      s_ref, work_ref, rank0_ref, rank1_ref, top0_ref, top1_ref,
                      cand_ref, cnt_ref):
    nk = PEER_N_KEYS
    q = q_ref[...]
    for hp in range(2 * PEER_HEADS):
        s_ref[hp * nk:(hp + 1) * nk, :] = lax.dot_general(
            keys_ref[hp], q[:, hp * PEER_HALF:(hp + 1) * PEER_HALF],
            (((1,), (1,)), ((), ())), preferred_element_type=F32)

    def scores(p, n):
        return s_ref[pl.ds(p * nk + n, PEER_HEADS, stride=2 * nk), :]

    for p, rank_ref, top_ref in ((0, rank0_ref, top0_ref), (1, rank1_ref, top1_ref)):
        for n in range(nk):
            work_ref[n] = scores(p, n)
            rank_ref[n] = jnp.full(work_ref.shape[1:], float(PEER_TOPK), F32)
        _extract_topk(work_ref, rank_ref, top_ref)

    for c, (k, l) in enumerate(_CANDS):
        cand_ref[c] = top0_ref[k] + top1_ref[l]
    for k in range(PEER_TOPK):
        cnt_ref[k] = jnp.zeros(cnt_ref.shape[1:], F32)
    cv0 = top0_ref[0] + top1_ref[0]
    big = float(PEER_TOPK * PEER_TOPK)

    def body(r, z):
        m = cand_ref[0]
        for c in range(1, len(_CANDS)):
            m = jnp.maximum(m, cand_ref[c])
        idx = jnp.full(m.shape, big, F32)
        for c, (k, l) in enumerate(_CANDS):
            idx = jnp.minimum(idx, jnp.where(cand_ref[c] == m, float(k * PEER_TOPK + l), big))
        for k in range(PEER_TOPK):
            cnt = cnt_ref[k]
            for c, (kc, l) in enumerate(_CANDS):
                if kc != k:
                    continue
                hit = idx == float(k * PEER_TOPK + l)
                cand_ref[c] = jnp.where(hit, -jnp.inf, cand_ref[c])
                cnt = cnt + jnp.where(hit, 1.0, 0.0)
            cnt_ref[k] = cnt
        return z + jnp.exp(m - cv0)

    z = lax.fori_loop(0, PEER_TOPK, body, jnp.zeros(cv0.shape, F32))
    zinv = 1.0 / z
    a0 = top0_ref[0]
    b0 = top1_ref[0]
    for n in range(nk):
        rows = pl.ds(n, PEER_HEADS, stride=nk)
        ra = rank0_ref[n]
        nsel = jnp.zeros(ra.shape, F32)
        for k in range(PEER_TOPK):
            nsel = jnp.where(ra == float(k), cnt_ref[k], nsel)
        nsel_ref[rows, :] = nsel
        coef_ref[rows, :] = jnp.exp(scores(0, n) - a0) * zinv
        rankb_ref[rows, :] = rank1_ref[n]
        e1_ref[rows, :] = jnp.exp(scores(1, n) - b0)


def _peer_topk(q, keys, tb):
    t = q.shape[0]
    rows = PEER_HEADS * PEER_N_KEYS
    out = jax.ShapeDtypeStruct((rows, t), F32)
    ospec = pl.BlockSpec((rows, tb), lambda i: (0, i))
    hv = (PEER_HEADS, tb)
    return pl.pallas_call(
        _peer_topk_kernel,
        grid=(t // tb,),
        in_specs=[pl.BlockSpec((tb, q.shape[1]), lambda i: (i, 0)),
                  pl.BlockSpec(keys.shape, lambda i: (0, 0, 0))],
        out_specs=[ospec, ospec, ospec, ospec],
        out_shape=[out, out, out, out],
        scratch_shapes=[pltpu.VMEM((2 * rows, tb), F32),
                        pltpu.VMEM((PEER_N_KEYS,) + hv, F32),
                        pltpu.VMEM((PEER_N_KEYS,) + hv, F32),
                        pltpu.VMEM((PEER_N_KEYS,) + hv, F32),
                        pltpu.VMEM((PEER_TOPK,) + hv, F32),
                        pltpu.VMEM((PEER_TOPK,) + hv, F32),
                        pltpu.VMEM((len(_CANDS),) + hv, F32),
                        pltpu.VMEM((PEER_TOPK,) + hv, F32)],
        compiler_params=_cparams("parallel"),
        name="peer_topk",
    )(q, keys)


def _peer_expert_kernel(xt_ref, u_ref, vt_ref, rankb_ref, e1_ref, nsel_ref, coef_ref,
                        xres_ref, gf_ref, y_ref, acc_ref, a0_ref, a1_ref, h0_ref, h1_ref,
                        *, n_i, n_tiles, final_norm):
    s = pl.program_id(1)
    nk = PEER_N_KEYS
    e_tile = n_i * nk

    @pl.when(s == 0)
    def _():
        acc_ref[...] = jnp.zeros_like(acc_ref)
        a1_ref[...] = jnp.zeros_like(a1_ref)
        h0_ref[...] = jnp.zeros_like(h0_ref)
        h1_ref[...] = jnp.zeros_like(h1_ref)

    def stage_a(half, a_dst):
        a_dst[...] = jnp.dot(u_ref[half * e_tile:(half + 1) * e_tile, :], xt_ref[...],
                             preferred_element_type=F32)

    def stage_b(tile, a_src, h_dst):
        tile = jnp.clip(tile, 0, n_tiles - 1)
        for il in range(n_i):
            i = tile * n_i + il
            gate = None
            for h in range(PEER_HEADS):
                hs = slice(h * nk, (h + 1) * nk)
                nsel_row = nsel_ref[pl.ds(h * nk + i, 1), :]
                coef_row = coef_ref[pl.ds(h * nk + i, 1), :]
                sel = jnp.where(rankb_ref[hs, :] < nsel_row, e1_ref[hs, :], 0.0) * coef_row
                gate = sel if gate is None else gate + sel
            rows = slice(il * nk, (il + 1) * nk)
            h_dst[rows, :] = (gate * jax.nn.gelu(a_src[rows, :])).astype(h_dst.dtype)

    def stage_c(half, h_src):
        acc_ref[...] += jnp.dot(vt_ref[:, half * e_tile:(half + 1) * e_tile], h_src[...],
                                preferred_element_type=F32)

    stage_c(0, h0_ref)
    stage_b(2 * s - 1, a1_ref, h1_ref)
    stage_a(0, a0_ref)

    stage_c(1, h1_ref)
    stage_b(2 * s, a0_ref, h0_ref)
    stage_a(1, a1_ref)

    @pl.when(s == pl.num_programs(1) - 1)
    def _():
        y = xres_ref[...] + acc_ref[...].T
        if final_norm:
            y = _rms(y, gf_ref[...])
        y_ref[...] = y


def _peer_experts(xn_t, u, v_t, rankb, e1, nsel, coef, xres, g_final, tb, e_tile, final_norm):
    d, t = xn_t.shape
    n_tiles = u.shape[0] // e_tile
    n_steps = n_tiles // 2 + 1
    rows = PEER_HEADS * PEER_N_KEYS
    aux = pl.BlockSpec((rows, tb), lambda i, j: (0, i))
    return pl.pallas_call(
        functools.partial(_peer_expert_kernel, n_i=e_tile // PEER_N_KEYS, n_tiles=n_tiles,
                          final_norm=final_norm),
        grid=(t // tb, n_steps),
        in_specs=[pl.BlockSpec((d, tb), lambda i, j: (0, i)),
                  pl.BlockSpec((2 * e_tile, d), lambda i, j: (jnp.minimum(j, n_steps - 2), 0)),
                  pl.BlockSpec((d, 2 * e_tile), lambda i, j: (0, jnp.maximum(j - 1, 0))),
                  aux, aux, aux, aux,
                  pl.BlockSpec((tb, d), lambda i, j: (i, 0), pipeline_mode=pl.Buffered(1)),
                  pl.BlockSpec((1, d), lambda i, j: (0, 0))],
        out_specs=pl.BlockSpec((tb, d), lambda i, j: (i, 0)),
        out_shape=jax.ShapeDtypeStruct((t, d), F32),
        scratch_shapes=[pltpu.VMEM((d, tb), F32),
                        pltpu.VMEM((e_tile, tb), F32),
                        pltpu.VMEM((e_tile, tb), F32),
                        pltpu.VMEM((e_tile, tb), BF16),
                        pltpu.VMEM((e_tile, tb), BF16)],
        compiler_params=_cparams("parallel", "arbitrary"),
        name="peer_experts",
    )(xn_t, u, v_t, rankb, e1, nsel, coef, xres, g_final.reshape(1, d))


def _row_tile(t, pref):
    return pref if t % pref == 0 else t


def _in_projection(x2, g_mix, w_in, sgu_g):
    t = x2.shape[0]
    xn = _norm_cast(x2, g_mix, _row_tile(t, 512))
    tm = _row_tile(t, 1024)
    c0, c1, c2, c3, c4 = D_ATTN, D_ATTN + 2 * D_KV, D_ATTN + 2 * D_KV + D_SGU, \
        D_ATTN + 2 * D_KV + 2 * D_SGU, w_in.shape[1]
    q = _mm(xn, w_in[:, :c0], "scale_q", BF16, tm, 1024, name="proj_q")
    kv = _mm(xn, w_in[:, c0:c1], "none", F32, tm, 2 * D_KV, name="proj_kv")
    u = _mm(xn, w_in[:, c1:c2], "gelu", BF16, tm, 1024, name="proj_u")
    vs = _mm(xn, w_in[:, c2:c3], "gelu_norm", F32, _row_tile(t, 512), D_SGU, g=sgu_g, name="proj_vsgu")
    gates = _mm(xn, w_in[:, c3:c4], "sigmoid", BF16, tm, 1024, name="proj_gates")
    return q, kv, u, vs, gates


def _peer(xn, xres, w_query, keys, u, v_t, g_final, final_norm):
    t = xn.shape[0]
    q = _mm(xn, w_query, "none", BF16, _row_tile(t, 1024), 1024, name="peer_query")
    rankb, e1, nsel, coef = _peer_topk(q, keys, LANES)
    tb = _row_tile(t, 512)
    return _peer_experts(xn.T, u, v_t, rankb, e1, nsel, coef, xres, g_final, tb, 256, final_norm)


def kernel(x_prompt, x_sample, cache_k_swa, cache_v_swa, norm_mix_g, w_in, sgu_norm_g, sgu_w_s,
           sgu_b_s, attn_sinks, rel_bias_table, w_branch_attn, w_branch_sgu, w_out, norm_ffn_g,
           peer_w_query, peer_sub_keys, peer_expert_u, peer_expert_v, norm_final_g):
    B, S, D = x_prompt.shape
    Bd, T, _ = x_sample.shape
    depth = w_in.shape[0]
    bias_p = _rel_bias(rel_bias_table, CHUNK, WINDOW + CHUNK)
    bias_s = _rel_bias(rel_bias_table, T, WINDOW + T)

    xp = x_prompt.reshape(B * S, D)
    xs = x_sample.reshape(Bd * T, D)
    nk_p, nv_p, nk_s, nv_s, nsgu_s = [], [], [], [], []
    for l in range(depth):
        last = l == depth - 1
        w_in_l = w_in[l].astype(BF16)
        w_pa = w_branch_attn[l].astype(BF16)
        w_pb = w_branch_sgu[l].astype(BF16)
        w_o = w_out[l].astype(BF16)
        w_q = peer_w_query[l].astype(BF16)
        keys = peer_sub_keys[l].astype(BF16).reshape(2 * PEER_HEADS, PEER_N_KEYS, PEER_HALF)
        e_u = peer_expert_u[l].astype(BF16)
        e_vt = peer_expert_v[l].astype(BF16).T
        g_out = norm_final_g if last else jnp.ones_like(norm_final_g)

        q, kv, u, vs, gates = _in_projection(xp, norm_mix_g[l], w_in_l, sgu_norm_g[l])
        kv3 = kv.reshape(B, S, 2 * D_KV)
        kpad = jnp.pad(kv3[..., :D_KV].astype(BF16), ((0, 0), (WINDOW, 0), (0, 0)))
        vpad = jnp.pad(kv3[..., D_KV:].astype(BF16), ((0, 0), (WINDOW, 0), (0, 0)))
        o = _attention(q.reshape(B, S, D_ATTN), kpad, vpad, bias_p, attn_sinks[l],
                       CHUNK, WINDOW + CHUNK, WINDOW)
        n_blk = S // SGU_BLOCK
        sg = _sgu(u.reshape(B * n_blk, SGU_BLOCK, D_SGU), vs.reshape(B * n_blk, SGU_BLOCK, D_SGU),
                  sgu_w_s[l], sgu_b_s[l], SGU_BLOCK)
        xp, xnp = _merge(o.reshape(B * S, D_ATTN), sg.reshape(B * S, D_SGU), gates, xp,
                         w_pa, w_pb, w_o, norm_ffn_g[l], _row_tile(B * S, 256))
        nk_p.append(kv3[:, S - WINDOW:, :D_KV].reshape(B, WINDOW, N_KV_HEADS, HEAD_DIM))
        nv_p.append(kv3[:, S - WINDOW:, D_KV:].reshape(B, WINDOW, N_KV_HEADS, HEAD_DIM))

        qs, kvs, us, vss, gates_s = _in_projection(xs, norm_mix_g[l], w_in_l, sgu_norm_g[l])
        kvs3 = kvs.reshape(Bd, T, 2 * D_KV)
        k_all = jnp.concatenate([cache_k_swa[l].reshape(Bd, WINDOW, D_KV).astype(BF16),
                                 kvs3[..., :D_KV].astype(BF16)], axis=1)
        v_all = jnp.concatenate([cache_v_swa[l].reshape(Bd, WINDOW, D_KV).astype(BF16),
                                 kvs3[..., D_KV:].astype(BF16)], axis=1)
        os_ = _attention(qs.reshape(Bd, T, D_ATTN), k_all, v_all, bias_s, attn_sinks[l],
                         T, WINDOW + T, 0)
        ss = _sgu(us.reshape(Bd, T, D_SGU), vss.reshape(Bd, T, D_SGU), sgu_w_s[l], sgu_b_s[l], T)
        xs, xns = _merge(os_.reshape(Bd * T, D_ATTN), ss.reshape(Bd * T, D_SGU), gates_s, xs,
                         w_pa, w_pb, w_o, norm_ffn_g[l], Bd * T)
        nk_s.append(kvs3[..., :D_KV].reshape(Bd, T, N_KV_HEADS, HEAD_DIM))
        nv_s.append(kvs3[..., D_KV:].reshape(Bd, T, N_KV_HEADS, HEAD_DIM))
        nsgu_s.append(vss.reshape(Bd, T, D_SGU))

        xp = _peer(xnp, xp, w_q, keys, e_u, e_vt, g_out, last)
        xs = _peer(xns, xs, w_q, keys, e_u, e_vt, g_out, last)

    return (xp.reshape(B, S, D), xs.reshape(Bd, T, D), jnp.stack(nk_p), jnp.stack(nv_p),
            jnp.stack(nk_s), jnp.stack(nv_s), jnp.stack(nsgu_s))
```

```python
import functools
import math

import jax
import jax.numpy as jnp
import numpy as np
from jax import lax
from jax.experimental import pallas as pl
from jax.experimental.pallas import tpu as pltpu

F32 = jnp.float32
BF16 = jnp.bfloat16

CHUNK = 64
WINDOW = 128
N_HEADS = 16
N_KV_HEADS = 2
HEAD_DIM = 64
GQA_GROUP = N_HEADS // N_KV_HEADS
D_ATTN = N_HEADS * HEAD_DIM
D_KV = N_KV_HEADS * HEAD_DIM
N_BUCKETS = 32
MAX_DISTANCE = 128
SGU_BLOCK = 128
SGU_GROUPS = 8
SGU_GROUP_DIM = 128
D_SGU = SGU_GROUPS * SGU_GROUP_DIM
PEER_HEADS = 8
PEER_N_KEYS = 128
PEER_HALF = 128
PEER_TOPK = 16
EPS = 1e-6
NEG_INF = -1e30

LANES = 128
SUBLANES = 8
VMEM_LIMIT_BYTES = 56 * 1024 * 1024

_CANDS = tuple((k, l) for k in range(PEER_TOPK) for l in range(PEER_TOPK)
               if (k + 1) * (l + 1) <= PEER_TOPK)


def _cparams(*sem):
    return pltpu.CompilerParams(dimension_semantics=sem, vmem_limit_bytes=VMEM_LIMIT_BYTES)


def _rms(x, g):
    r = lax.rsqrt(jnp.mean(x * x, axis=-1, keepdims=True) + EPS)
    return x * r * g


def _norm_kernel(x_ref, g_ref, o_ref):
    o_ref[...] = _rms(x_ref[...], g_ref[...]).astype(o_ref.dtype)


def _norm_cast(x, g, tm):
    t, d = x.shape
    return pl.pallas_call(
        _norm_kernel,
        grid=(t // tm,),
        in_specs=[pl.BlockSpec((tm, d), lambda i: (i, 0)),
                  pl.BlockSpec((1, d), lambda i: (0, 0))],
        out_specs=pl.BlockSpec((tm, d), lambda i: (i, 0)),
        out_shape=jax.ShapeDtypeStruct((t, d), BF16),
        compiler_params=_cparams("parallel"),
        name="rmsnorm_cast",
    )(x, g.reshape(1, d))


def _mm_kernel(x_ref, w_ref, *rest, epilogue):
    acc = jnp.dot(x_ref[...], w_ref[...], preferred_element_type=F32)
    if epilogue == "scale_q":
        (o_ref,) = rest
        o_ref[...] = (acc * (HEAD_DIM ** -0.5)).astype(o_ref.dtype)
    elif epilogue == "none":
        (o_ref,) = rest
        o_ref[...] = acc.astype(o_ref.dtype)
    elif epilogue == "gelu":
        (o_ref,) = rest
        o_ref[...] = jax.nn.gelu(acc).astype(o_ref.dtype)
    elif epilogue == "gelu_norm":
        g_ref, o_ref = rest
        o_ref[...] = _rms(jax.nn.gelu(acc), g_ref[...]).astype(o_ref.dtype)
    elif epilogue == "sigmoid":
        (o_ref,) = rest
        o_ref[...] = jax.nn.sigmoid(acc).astype(o_ref.dtype)
    else:
        raise ValueError(epilogue)


def _mm(x, w, epilogue, out_dtype, tm, tn, g=None, name="mm"):
    t, k = x.shape
    n = w.shape[1]
    in_specs = [pl.BlockSpec((tm, k), lambda i, j: (i, 0)),
                pl.BlockSpec((k, tn), lambda i, j: (0, j))]
    args = [x, w]
    if g is not None:
        in_specs.append(pl.BlockSpec((1, tn), lambda i, j: (0, j)))
        args.append(g.reshape(1, n))
    return pl.pallas_call(
        functools.partial(_mm_kernel, epilogue=epilogue),
        grid=(t // tm, n // tn),
        in_specs=in_specs,
        out_specs=pl.BlockSpec((tm, tn), lambda i, j: (i, j)),
        out_shape=jax.ShapeDtypeStruct((t, n), out_dtype),
        compiler_params=_cparams("parallel", "parallel"),
        name=name,
    )(*args)


def _bias_kernel(table_ref, bucket_ref, o_ref):
    bucket = bucket_ref[...]
    for h in range(N_HEADS):
        acc = jnp.zeros(bucket.shape, F32)
        for b in range(N_BUCKETS):
            acc = jnp.where(bucket == b, table_ref[b, h], acc)
        o_ref[h] = acc


def _t5_bucket(rel):
    half = N_BUCKETS // 2
    max_exact = half // 2
    ret = jnp.where(rel > 0, half, 0)
    n = jnp.abs(rel)
    nf = jnp.maximum(n, 1).astype(jnp.float32)
    large = max_exact + (jnp.log(nf / max_exact) / math.log(MAX_DISTANCE / max_exact)
                         * (half - max_exact)).astype(jnp.int32)
    large = jnp.minimum(large, half - 1)
    return ret + jnp.where(n < max_exact, n, large)


def _rel_bias(table, n_q, n_k):
    rel = (jnp.arange(n_k, dtype=jnp.int32)[None, :] - WINDOW
           - jnp.arange(n_q, dtype=jnp.int32)[:, None])
    bucket = _t5_bucket(rel).astype(jnp.int32)
    return pl.pallas_call(
        _bias_kernel,
        in_specs=[pl.BlockSpec(memory_space=pltpu.SMEM),
                  pl.BlockSpec(memory_space=pltpu.VMEM)],
        out_specs=pl.BlockSpec(memory_space=pltpu.VMEM),
        out_shape=jax.ShapeDtypeStruct((N_HEADS, n_q, n_k), F32),
        name="rel_bias",
    )(table.astype(F32), bucket)


def _attn_kernel(sinks_ref, q_ref, k_ref, v_ref, bias_ref, o_ref, *, n_q, n_k, n_pad):
    c = pl.program_id(1)
    start = pl.multiple_of(c * n_q, n_q)
    kb = k_ref[0, pl.ds(start, n_k), :]
    vb = v_ref[0, pl.ds(start, n_k), :]
    key_row = start + lax.broadcasted_iota(jnp.int32, (1, n_k), 1)
    valid = key_row >= n_pad
    heads = range(N_HEADS)
    kv_cols = [slice((h // GQA_GROUP) * HEAD_DIM, (h // GQA_GROUP + 1) * HEAD_DIM) for h in heads]
    s = [lax.dot_general(q_ref[0, :, h * HEAD_DIM:(h + 1) * HEAD_DIM], kb[:, kv_cols[h]],
                         (((1,), (1,)), ((), ())), preferred_element_type=F32) for h in heads]
    s = [jnp.where(valid, s[h] + bias_ref[h], NEG_INF) for h in heads]
    m = [jnp.maximum(jnp.max(s[h], axis=-1, keepdims=True), sinks_ref[h]) for h in heads]
    p = [jnp.exp(s[h] - m[h]) for h in heads]
    denom = [jnp.sum(p[h], axis=-1, keepdims=True) + jnp.exp(sinks_ref[h] - m[h]) for h in heads]
    o = [jnp.dot(p[h].astype(BF16), vb[:, kv_cols[h]], preferred_element_type=F32) for h in heads]
    for h in heads:
        o_ref[0, :, h * HEAD_DIM:(h + 1) * HEAD_DIM] = (o[h] / denom[h]).astype(o_ref.dtype)


def _attention(q, k, v, bias, sinks, n_q, n_k, n_pad):
    nb, sq, _ = q.shape
    rows = k.shape[1]
    return pl.pallas_call(
        functools.partial(_attn_kernel, n_q=n_q, n_k=n_k, n_pad=n_pad),
        grid=(nb, sq // n_q),
        in_specs=[pl.BlockSpec(memory_space=pltpu.SMEM),
                  pl.BlockSpec((1, n_q, D_ATTN), lambda b, c: (b, c, 0)),
                  pl.BlockSpec((1, rows, D_KV), lambda b, c: (b, 0, 0)),
                  pl.BlockSpec((1, rows, D_KV), lambda b, c: (b, 0, 0)),
                  pl.BlockSpec((N_HEADS, n_q, n_k), lambda b, c: (0, 0, 0))],
        out_specs=pl.BlockSpec((1, n_q, D_ATTN), lambda b, c: (b, c, 0)),
        out_shape=jax.ShapeDtypeStruct((nb, sq, D_ATTN), BF16),
        compiler_params=_cparams("parallel", "arbitrary"),
        name="swa_attention",
    )(sinks.astype(F32), q, k, v, bias)


def _sgu_kernel(u_ref, v_ref, w_ref, b_ref, o_ref, *, rows):
    r = lax.broadcasted_iota(jnp.int32, (rows, rows), 0)
    c = lax.broadcasted_iota(jnp.int32, (rows, rows), 1)
    tril = c <= r
    for g in range(SGU_GROUPS):
        sl = slice(g * SGU_GROUP_DIM, (g + 1) * SGU_GROUP_DIM)
        w = jnp.where(tril, w_ref[g], 0.0).astype(BF16)
        mixed = jnp.dot(w, v_ref[0, :, sl].astype(BF16), preferred_element_type=F32)
        o_ref[0, :, sl] = (u_ref[0, :, sl].astype(F32) * (mixed + b_ref[:, sl])).astype(o_ref.dtype)


def _sgu(u, vn, w_s, b_s, rows):
    nblk = u.shape[0]
    w = w_s[:, :rows, :rows]
    b_full = jnp.repeat(b_s[:, :rows].T, SGU_GROUP_DIM, axis=1).astype(F32)
    return pl.pallas_call(
        functools.partial(_sgu_kernel, rows=rows),
        grid=(nblk,),
        in_specs=[pl.BlockSpec((1, rows, D_SGU), lambda i: (i, 0, 0)),
                  pl.BlockSpec((1, rows, D_SGU), lambda i: (i, 0, 0)),
                  pl.BlockSpec((SGU_GROUPS, rows, rows), lambda i: (0, 0, 0)),
                  pl.BlockSpec((rows, D_SGU), lambda i: (0, 0))],
        out_specs=pl.BlockSpec((1, rows, D_SGU), lambda i: (i, 0, 0)),
        out_shape=jax.ShapeDtypeStruct((nblk, rows, D_SGU), BF16),
        compiler_params=_cparams("parallel"),
        name="sgu",
    )(u, vn, w, b_full)


def _merge_kernel(o_ref, s_ref, gate_ref, x_ref, wpa_ref, wpb_ref, wout_ref, gn_ref,
                  xo_ref, xn_ref):
    d = x_ref.shape[-1]
    ta = jnp.dot(o_ref[...], wpa_ref[...], preferred_element_type=F32)
    tb = jnp.dot(s_ref[...], wpb_ref[...], preferred_element_type=F32)
    h = gate_ref[:, :d].astype(F32) * ta + gate_ref[:, d:].astype(F32) * tb
    y = x_ref[...] + jnp.dot(h.astype(BF16), wout_ref[...], preferred_element_type=F32)
    xo_ref[...] = y
    xn_ref[...] = _rms(y, gn_ref[...]).astype(xn_ref.dtype)


def _merge(o, s, gates, x, w_pa, w_pb, w_out, g_ffn, tm):
    t, d = x.shape
    once = pl.Buffered(1)
    return pl.pallas_call(
        _merge_kernel,
        grid=(t // tm,),
        in_specs=[pl.BlockSpec((tm, D_ATTN), lambda i: (i, 0)),
                  pl.BlockSpec((tm, D_SGU), lambda i: (i, 0)),
                  pl.BlockSpec((tm, 2 * d), lambda i: (i, 0)),
                  pl.BlockSpec((tm, d), lambda i: (i, 0)),
                  pl.BlockSpec((D_ATTN, d), lambda i: (0, 0), pipeline_mode=once),
                  pl.BlockSpec((D_SGU, d), lambda i: (0, 0), pipeline_mode=once),
                  pl.BlockSpec((d, d), lambda i: (0, 0), pipeline_mode=once),
                  pl.BlockSpec((1, d), lambda i: (0, 0))],
        out_specs=[pl.BlockSpec((tm, d), lambda i: (i, 0)),
                   pl.BlockSpec((tm, d), lambda i: (i, 0))],
        out_shape=[jax.ShapeDtypeStruct((t, d), F32),
                   jax.ShapeDtypeStruct((t, d), BF16)],
        compiler_params=_cparams("parallel"),
        name="merge",
    )(o, s, gates, x, w_pa, w_pb, w_out, g_ffn.reshape(1, d))


def _extract_topk(work_ref, rank_ref, top_ref):
    n_keys = work_ref.shape[0]

    def body(r, carry):
        m = work_ref[0]
        for n in range(1, n_keys):
            m = jnp.maximum(m, work_ref[n])
        idx = jnp.full(m.shape, float(n_keys), F32)
        for n in range(n_keys):
            idx = jnp.minimum(idx, jnp.where(work_ref[n] == m, float(n), float(n_keys)))
        rf = jnp.asarray(r, F32)
        for n in range(n_keys):
            hit = idx == float(n)
            work_ref[n] = jnp.where(hit, -jnp.inf, work_ref[n])
            rank_ref[n] = jnp.where(hit, rf, rank_ref[n])
        top_ref[r] = m
        return carry

    lax.fori_loop(0, PEER_TOPK, body, 0)


def _peer_topk_kernel(q_ref, keys_ref, rankb_ref, e1_ref, nsel_ref, coef_ref,
                      s_ref, work_ref, rank0_ref, rank1_ref, top0_ref, top1_ref,
                      cand_ref, cnt_ref):
    nk = PEER_N_KEYS
    q = q_ref[...]
    for hp in range(2 * PEER_HEADS):
        s_ref[hp * nk:(hp + 1) * nk, :] = lax.dot_general(
            keys_ref[hp], q[:, hp * PEER_HALF:(hp + 1) * PEER_HALF],
            (((1,), (1,)), ((), ())), preferred_element_type=F32)

    def scores(p, n):
        return s_ref[pl.ds(p * nk + n, PEER_HEADS, stride=2 * nk), :]

    for p, rank_ref, top_ref in ((0, rank0_ref, top0_ref), (1, rank1_ref, top1_ref)):
        for n in range(nk):
            work_ref[n] = scores(p, n)
            rank_ref[n] = jnp.full(work_ref.shape[1:], float(PEER_TOPK), F32)
        _extract_topk(work_ref, rank_ref, top_ref)

    for c, (k, l) in enumerate(_CANDS):
        cand_ref[c] = top0_ref[k] + top1_ref[l]
    for k in range(PEER_TOPK):
        cnt_ref[k] = jnp.zeros(cnt_ref.shape[1:], F32)
    cv0 = top0_ref[0] + top1_ref[0]
    big = float(PEER_TOPK * PEER_TOPK)

    def body(r, z):
        m = cand_ref[0]
        for c in range(1, len(_CANDS)):
            m = jnp.maximum(m, cand_ref[c])
        idx = jnp.full(m.shape, big, F32)
        for c, (k, l) in enumerate(_CANDS):
            idx = jnp.minimum(idx, jnp.where(cand_ref[c] == m, float(k * PEER_TOPK + l), big))
        for k in range(PEER_TOPK):
            cnt = cnt_ref[k]
            for c, (kc, l) in enumerate(_CANDS):
                if kc != k:
                    continue
                hit = idx == float(k * PEER_TOPK + l)
                cand_ref[c] = jnp.where(hit, -jnp.inf, cand_ref[c])
                cnt = cnt + jnp.where(hit, 1.0, 0.0)
            cnt_ref[k] = cnt
        return z + jnp.exp(m - cv0)

    z = lax.fori_loop(0, PEER_TOPK, body, jnp.zeros(cv0.shape, F32))
    zinv = 1.0 / z
    a0 = top0_ref[0]
    b0 = top1_ref[0]
    for n in range(nk):
        rows = pl.ds(n, PEER_HEADS, stride=nk)
        ra = rank0_ref[n]
        nsel = jnp.zeros(ra.shape, F32)
        for k in range(PEER_TOPK):
            nsel = jnp.where(ra == float(k), cnt_ref[k], nsel)
        nsel_ref[rows, :] = nsel
        coef_ref[rows, :] = jnp.exp(scores(0, n) - a0) * zinv
        rankb_ref[rows, :] = rank1_ref[n]
        e1_ref[rows, :] = jnp.exp(scores(1, n) - b0)


def _peer_topk(q, keys, tb):
    t = q.shape[0]
    rows = PEER_HEADS * PEER_N_KEYS
    out = jax.ShapeDtypeStruct((rows, t), F32)
    ospec = pl.BlockSpec((rows, tb), lambda i: (0, i))
    hv = (PEER_HEADS, tb)
    return pl.pallas_call(
        _peer_topk_kernel,
        grid=(t // tb,),
        in_specs=[pl.BlockSpec((tb, q.shape[1]), lambda i: (i, 0)),
                  pl.BlockSpec(keys.shape, lambda i: (0, 0, 0))],
        out_specs=[ospec, ospec, ospec, ospec],
        out_shape=[out, out, out, out],
        scratch_shapes=[pltpu.VMEM((2 * rows, tb), F32),
                        pltpu.VMEM((PEER_N_KEYS,) + hv, F32),
                        pltpu.VMEM((PEER_N_KEYS,) + hv, F32),
                        pltpu.VMEM((PEER_N_KEYS,) + hv, F32),
                        pltpu.VMEM((PEER_TOPK,) + hv, F32),
                        pltpu.VMEM((PEER_TOPK,) + hv, F32),
                        pltpu.VMEM((len(_CANDS),) + hv, F32),
                        pltpu.VMEM((PEER_TOPK,) + hv, F32)],
        compiler_params=_cparams("parallel"),
        name="peer_topk",
    )(q, keys)


def _peer_expert_kernel(xt_ref, u_ref, vt_ref, rankb_ref, e1_ref, nsel_ref, coef_ref,
                        xres_ref, gf_ref, y_ref, acc_ref, a0_ref, a1_ref, h0_ref, h1_ref,
                        *, n_i, n_tiles, final_norm):
    s = pl.program_id(1)
    nk = PEER_N_KEYS
    e_tile = n_i * nk

    @pl.when(s == 0)
    def _():
        acc_ref[...] = jnp.zeros_like(acc_ref)
        a1_ref[...] = jnp.zeros_like(a1_ref)
        h0_ref[...] = jnp.zeros_like(h0_ref)
        h1_ref[...] = jnp.zeros_like(h1_ref)

    def stage_a(half, a_dst):
        a_dst[...] = jnp.dot(u_ref[half * e_tile:(half + 1) * e_tile, :], xt_ref[...],
                             preferred_element_type=F32)

    def stage_b(tile, a_src, h_dst):
        tile = jnp.clip(tile, 0, n_tiles - 1)
        for il in range(n_i):
            i = tile * n_i + il
            gate = None
            for h in range(PEER_HEADS):
                hs = slice(h * nk, (h + 1) * nk)
                nsel_row = nsel_ref[pl.ds(h * nk + i, 1), :]
                coef_row = coef_ref[pl.ds(h * nk + i, 1), :]
                sel = jnp.where(rankb_ref[hs, :] < nsel_row, e1_ref[hs, :], 0.0) * coef_row
                gate = sel if gate is None else gate + sel
            rows = slice(il * nk, (il + 1) * nk)
            h_dst[rows, :] = (gate * jax.nn.gelu(a_src[rows, :])).astype(h_dst.dtype)

    def stage_c(half, h_src):
        acc_ref[...] += jnp.dot(vt_ref[:, half * e_tile:(half + 1) * e_tile], h_src[...],
                                preferred_element_type=F32)

    stage_c(0, h0_ref)
    stage_b(2 * s - 1, a1_ref, h1_ref)
    stage_a(0, a0_ref)

    stage_c(1, h1_ref)
    stage_b(2 * s, a0_ref, h0_ref)
    stage_a(1, a1_ref)

    @pl.when(s == pl.num_programs(1) - 1)
    def _():
        y = xres_ref[...] + acc_ref[...].T
        if final_norm:
            y = _rms(y, gf_ref[...])
        y_ref[...] = y


def _peer_experts(xn_t, u, v_t, rankb, e1, nsel, coef, xres, g_final, tb, e_tile, final_norm):
    d, t = xn_t.shape
    n_tiles = u.shape[0] // e_tile
    n_steps = n_tiles // 2 + 1
    rows = PEER_HEADS * PEER_N_KEYS
    aux = pl.BlockSpec((rows, tb), lambda i, j: (0, i))
    return pl.pallas_call(
        functools.partial(_peer_expert_kernel, n_i=e_tile // PEER_N_KEYS, n_tiles=n_tiles,
                          final_norm=final_norm),
        grid=(t // tb, n_steps),
        in_specs=[pl.BlockSpec((d, tb), lambda i, j: (0, i)),
                  pl.BlockSpec((2 * e_tile, d), lambda i, j: (jnp.minimum(j, n_steps - 2), 0)),
                  pl.BlockSpec((d, 2 * e_tile), lambda i, j: (0, jnp.maximum(j - 1, 0))),
                  aux, aux, aux, aux,
                  pl.BlockSpec((tb, d), lambda i, j: (i, 0), pipeline_mode=pl.Buffered(1)),
                  pl.BlockSpec((1, d), lambda i, j: (0, 0))],
        out_specs=pl.BlockSpec((tb, d), lambda i, j: (i, 0)),
        out_shape=jax.ShapeDtypeStruct((t, d), F32),
        scratch_shapes=[pltpu.VMEM((d, tb), F32),
                        pltpu.VMEM((e_tile, tb), F32),
                        pltpu.VMEM((e_tile, tb), F32),
                        pltpu.VMEM((e_tile, tb), BF16),
                        pltpu.VMEM((e_tile, tb), BF16)],
        compiler_params=_cparams("parallel", "arbitrary"),
        name="peer_experts",
    )(xn_t, u, v_t, rankb, e1, nsel, coef, xres, g_final.reshape(1, d))


def _row_tile(t, pref):
    return pref if t % pref == 0 else t


def _in_projection(x2, g_mix, w_in, sgu_g):
    t = x2.shape[0]
    xn = _norm_cast(x2, g_mix, _row_tile(t, 512))
    tm = _row_tile(t, 1024)
    c0, c1, c2, c3, c4 = D_ATTN, D_ATTN + 2 * D_KV, D_ATTN + 2 * D_KV + D_SGU, \
        D_ATTN + 2 * D_KV + 2 * D_SGU, w_in.shape[1]
    q = _mm(xn, w_in[:, :c0], "scale_q", BF16, tm, 1024, name="proj_q")
    kv = _mm(xn, w_in[:, c0:c1], "none", F32, tm, 2 * D_KV, name="proj_kv")
    u = _mm(xn, w_in[:, c1:c2], "gelu", BF16, tm, 1024, name="proj_u")
    vs = _mm(xn, w_in[:, c2:c3], "gelu_norm", F32, _row_tile(t, 512), D_SGU, g=sgu_g, name="proj_vsgu")
    gates = _mm(xn, w_in[:, c3:c4], "sigmoid", BF16, tm, 1024, name="proj_gates")
    return q, kv, u, vs, gates


def _peer(xn, xres, w_query, keys, u, v_t, g_final, final_norm):
    t = xn.shape[0]
    q = _mm(xn, w_query, "none", BF16, _row_tile(t, 1024), 1024, name="peer_query")
    rankb, e1, nsel, coef = _peer_topk(q, keys, LANES)
    tb = _row_tile(t, 512)
    return _peer_experts(xn.T, u, v_t, rankb, e1, nsel, coef, xres, g_final, tb, 256, final_norm)


def kernel(x_prompt, x_sample, cache_k_swa, cache_v_swa, norm_mix_g, w_in, sgu_norm_g, sgu_w_s,
           sgu_b_s, attn_sinks, rel_bias_table, w_branch_attn, w_branch_sgu, w_out, norm_ffn_g,
           peer_w_query, peer_sub_keys, peer_expert_u, peer_expert_v, norm_final_g):
    B, S, D = x_prompt.shape
    Bd, T, _ = x_sample.shape
    depth = w_in.shape[0]
    bias_p = _rel_bias(rel_bias_table, CHUNK, WINDOW + CHUNK)
    bias_s = _rel_bias(rel_bias_table, T, WINDOW + T)

    xp = x_prompt.reshape(B * S, D)
    xs = x_sample.reshape(Bd * T, D)
    nk_p, nv_p, nk_s, nv_s, nsgu_s = [], [], [], [], []
    for l in range(depth):
        last = l == depth - 1
        w_in_l = w_in[l].astype(BF16)
        w_pa = w_branch_attn[l].astype(BF16)
        w_pb = w_branch_sgu[l].astype(BF16)
        w_o = w_out[l].astype(BF16)
        w_q = peer_w_query[l].astype(BF16)
        keys = peer_sub_keys[l].astype(BF16).reshape(2 * PEER_HEADS, PEER_N_KEYS, PEER_HALF)
        e_u = peer_expert_u[l].astype(BF16)
        e_vt = peer_expert_v[l].astype(BF16).T
        g_out = norm_final_g if last else jnp.ones_like(norm_final_g)

        q, kv, u, vs, gates = _in_projection(xp, norm_mix_g[l], w_in_l, sgu_norm_g[l])
        kv3 = kv.reshape(B, S, 2 * D_KV)
        kpad = jnp.pad(kv3[..., :D_KV].astype(BF16), ((0, 0), (WINDOW, 0), (0, 0)))
        vpad = jnp.pad(kv3[..., D_KV:].astype(BF16), ((0, 0), (WINDOW, 0), (0, 0)))
        o = _attention(q.reshape(B, S, D_ATTN), kpad, vpad, bias_p, attn_sinks[l],
                       CHUNK, WINDOW + CHUNK, WINDOW)
        n_blk = S // SGU_BLOCK
        sg = _sgu(u.reshape(B * n_blk, SGU_BLOCK, D_SGU), vs.reshape(B * n_blk, SGU_BLOCK, D_SGU),
                  sgu_w_s[l], sgu_b_s[l], SGU_BLOCK)
        xp, xnp = _merge(o.reshape(B * S, D_ATTN), sg.reshape(B * S, D_SGU), gates, xp,
                         w_pa, w_pb, w_o, norm_ffn_g[l], _row_tile(B * S, 256))
        nk_p.append(kv3[:, S - WINDOW:, :D_KV].reshape(B, WINDOW, N_KV_HEADS, HEAD_DIM))
        nv_p.append(kv3[:, S - WINDOW:, D_KV:].reshape(B, WINDOW, N_KV_HEADS, HEAD_DIM))

        qs, kvs, us, vss, gates_s = _in_projection(xs, norm_mix_g[l], w_in_l, sgu_norm_g[l])
        kvs3 = kvs.reshape(Bd, T, 2 * D_KV)
        k_all = jnp.concatenate([cache_k_swa[l].reshape(Bd, WINDOW, D_KV).astype(BF16),
                                 kvs3[..., :D_KV].astype(BF16)], axis=1)
        v_all = jnp.concatenate([cache_v_swa[l].reshape(Bd, WINDOW, D_KV).astype(BF16),
                                 kvs3[..., D_KV:].astype(BF16)], axis=1)
        os_ = _attention(qs.reshape(Bd, T, D_ATTN), k_all, v_all, bias_s, attn_sinks[l],
                         T, WINDOW + T, 0)
        ss = _sgu(us.reshape(Bd, T, D_SGU), vss.reshape(Bd, T, D_SGU), sgu_w_s[l], sgu_b_s[l], T)
        xs, xns = _merge(os_.reshape(Bd * T, D_ATTN), ss.reshape(Bd * T, D_SGU), gates_s, xs,
                         w_pa, w_pb, w_o, norm_ffn_g[l], Bd * T)
        nk_s.append(kvs3[..., :D_KV].reshape(Bd, T, N_KV_HEADS, HEAD_DIM))
        nv_s.append(kvs3[..., D_KV:].reshape(Bd, T, N_KV_HEADS, HEAD_DIM))
        nsgu_s.append(vss.reshape(Bd, T, D_SGU))

        xp = _peer(xnp, xp, w_q, keys, e_u, e_vt, g_out, last)
        xs = _peer(xns, xs, w_q, keys, e_u, e_vt, g_out, last)

    return (xp.reshape(B, S, D), xs.reshape(Bd, T, D), jnp.stack(nk_p), jnp.stack(nv_p),
            jnp.stack(nk_s), jnp.stack(nv_s), jnp.stack(nsgu_s))
```

```python
import functools
import math

import jax
import jax.numpy as jnp
import numpy as np
from jax import lax
from jax.experimental import pallas as pl
from jax.experimental.pallas import tpu as pltpu

F32 = jnp.float32
BF16 = jnp.bfloat16

CHUNK = 64
WINDOW = 128
N_HEADS = 16
N_KV_HEADS = 2
HEAD_DIM = 64
GQA_GROUP = N_HEADS // N_KV_HEADS
D_ATTN = N_HEADS * HEAD_DIM
D_KV = N_KV_HEADS * HEAD_DIM
N_BUCKETS = 32
MAX_DISTANCE = 128
SGU_BLOCK = 128
SGU_GROUPS = 8
SGU_GROUP_DIM = 128
D_SGU = SGU_GROUPS * SGU_GROUP_DIM
PEER_HEADS = 8
PEER_N_KEYS = 128
PEER_HALF = 128
PEER_TOPK = 16
EPS = 1e-6
NEG_INF = -1e30

LANES = 128
SUBLANES = 8
VMEM_LIMIT_BYTES = 56 * 1024 * 1024

_CANDS = tuple((k, l) for k in range(PEER_TOPK) for l in range(PEER_TOPK)
               if (k + 1) * (l + 1) <= PEER_TOPK)


def _cparams(*sem):
    return pltpu.CompilerParams(dimension_semantics=sem, vmem_limit_bytes=VMEM_LIMIT_BYTES)


def _rms(x, g):
    r = lax.rsqrt(jnp.mean(x * x, axis=-1, keepdims=True) + EPS)
    return x * r * g


def _norm_kernel(x_ref, g_ref, o_ref):
    o_ref[...] = _rms(x_ref[...], g_ref[...]).astype(o_ref.dtype)


def _norm_cast(x, g, tm):
    t, d = x.shape
    return pl.pallas_call(
        _norm_kernel,
        grid=(t // tm,),
        in_specs=[pl.BlockSpec((tm, d), lambda i: (i, 0)),
                  pl.BlockSpec((1, d), lambda i: (0, 0))],
        out_specs=pl.BlockSpec((tm, d), lambda i: (i, 0)),
        out_shape=jax.ShapeDtypeStruct((t, d), BF16),
        compiler_params=_cparams("parallel"),
        name="rmsnorm_cast",
    )(x, g.reshape(1, d))


def _mm_kernel(x_ref, w_ref, *rest, epilogue):
    acc = jnp.dot(x_ref[...], w_ref[...], preferred_element_type=F32)
    if epilogue == "scale_q":
        (o_ref,) = rest
        o_ref[...] = (acc * (HEAD_DIM ** -0.5)).astype(o_ref.dtype)
    elif epilogue == "none":
        (o_ref,) = rest
        o_ref[...] = acc.astype(o_ref.dtype)
    elif epilogue == "gelu":
        (o_ref,) = rest
        o_ref[...] = jax.nn.gelu(acc).astype(o_ref.dtype)
    elif epilogue == "gelu_norm":
        g_ref, o_ref = rest
        o_ref[...] = _rms(jax.nn.gelu(acc), g_ref[...]).astype(o_ref.dtype)
    elif epilogue == "sigmoid":
        (o_ref,) = rest
        o_ref[...] = jax.nn.sigmoid(acc).astype(o_ref.dtype)
    else:
        raise ValueError(epilogue)


def _mm(x, w, epilogue, out_dtype, tm, tn, g=None, name="mm"):
    t, k = x.shape
    n = w.shape[1]
    in_specs = [pl.BlockSpec((tm, k), lambda i, j: (i, 0)),
                pl.BlockSpec((k, tn), lambda i, j: (0, j))]
    args = [x, w]
    if g is not None:
        in_specs.append(pl.BlockSpec((1, tn), lambda i, j: (0, j)))
        args.append(g.reshape(1, n))
    return pl.pallas_call(
        functools.partial(_mm_kernel, epilogue=epilogue),
        grid=(t // tm, n // tn),
        in_specs=in_specs,
        out_specs=pl.BlockSpec((tm, tn), lambda i, j: (i, j)),
        out_shape=jax.ShapeDtypeStruct((t, n), out_dtype),
        compiler_params=_cparams("parallel", "parallel"),
        name=name,
    )(*args)


def _bias_kernel(table_ref, bucket_ref, o_ref):
    bucket = bucket_ref[...]
    for h in range(N_HEADS):
        acc = jnp.zeros(bucket.shape, F32)
        for b in range(N_BUCKETS):
            acc = jnp.where(bucket == b, table_ref[b, h], acc)
        o_ref[h] = acc


def _t5_bucket(rel):
    half = N_BUCKETS // 2
    max_exact = half // 2
    ret = jnp.where(rel > 0, half, 0)
    n = jnp.abs(rel)
    nf = jnp.maximum(n, 1).astype(jnp.float32)
    large = max_exact + (jnp.log(nf / max_exact) / math.log(MAX_DISTANCE / max_exact)
                         * (half - max_exact)).astype(jnp.int32)
    large = jnp.minimum(large, half - 1)
    return ret + jnp.where(n < max_exact, n, large)


def _rel_bias(table, n_q, n_k):
    rel = (jnp.arange(n_k, dtype=jnp.int32)[None, :] - WINDOW
           - jnp.arange(n_q, dtype=jnp.int32)[:, None])
    bucket = _t5_bucket(rel).astype(jnp.int32)
    return pl.pallas_call(
        _bias_kernel,
        in_specs=[pl.BlockSpec(memory_space=pltpu.SMEM),
                  pl.BlockSpec(memory_space=pltpu.VMEM)],
        out_specs=pl.BlockSpec(memory_space=pltpu.VMEM),
        out_shape=jax.ShapeDtypeStruct((N_HEADS, n_q, n_k), F32),
        name="rel_bias",
    )(table.astype(F32), bucket)


def _attn_kernel(sinks_ref, q_ref, k_ref, v_ref, bias_ref, o_ref, *, n_q, n_k, n_pad):
    c = pl.program_id(1)
    start = pl.multiple_of(c * n_q, n_q)
    kb = k_ref[0, pl.ds(start, n_k), :]
    vb = v_ref[0, pl.ds(start, n_k), :]
    key_row = start + lax.broadcasted_iota(jnp.int32, (1, n_k), 1)
    valid = key_row >= n_pad
    heads = range(N_HEADS)
    kv_cols = [slice((h // GQA_GROUP) * HEAD_DIM, (h // GQA_GROUP + 1) * HEAD_DIM) for h in heads]
    s = [lax.dot_general(q_ref[0, :, h * HEAD_DIM:(h + 1) * HEAD_DIM], kb[:, kv_cols[h]],
                         (((1,), (1,)), ((), ())), preferred_element_type=F32) for h in heads]
    s = [jnp.where(valid, s[h] + bias_ref[h], NEG_INF) for h in heads]
    m = [jnp.maximum(jnp.max(s[h], axis=-1, keepdims=True), sinks_ref[h]) for h in heads]
    p = [jnp.exp(s[h] - m[h]) for h in heads]
    denom = [jnp.sum(p[h], axis=-1, keepdims=True) + jnp.exp(sinks_ref[h] - m[h]) for h in heads]
    o = [jnp.dot(p[h].astype(BF16), vb[:, kv_cols[h]], preferred_element_type=F32) for h in heads]
    for h in heads:
        o_ref[0, :, h * HEAD_DIM:(h + 1) * HEAD_DIM] = (o[h] / denom[h]).astype(o_ref.dtype)


def _attention(q, k, v, bias, sinks, n_q, n_k, n_pad):
    nb, sq, _ = q.shape
    rows = k.shape[1]
    return pl.pallas_call(
        functools.partial(_attn_kernel, n_q=n_q, n_k=n_k, n_pad=n_pad),
        grid=(nb, sq // n_q),
        in_specs=[pl.BlockSpec(memory_space=pltpu.SMEM),
                  pl.BlockSpec((1, n_q, D_ATTN), lambda b, c: (b, c, 0)),
                  pl.BlockSpec((1, rows, D_KV), lambda b, c: (b, 0, 0)),
                  pl.BlockSpec((1, rows, D_KV), lambda b, c: (b, 0, 0)),
                  pl.BlockSpec((N_HEADS, n_q, n_k), lambda b, c: (0, 0, 0))],
        out_specs=pl.BlockSpec((1, n_q, D_ATTN), lambda b, c: (b, c, 0)),
        out_shape=jax.ShapeDtypeStruct((nb, sq, D_ATTN), BF16),
        compiler_params=_cparams("parallel", "arbitrary"),
        name="swa_attention",
    )(sinks.astype(F32), q, k, v, bias)


def _sgu_kernel(u_ref, v_ref, w_ref, b_ref, o_ref, *, rows):
    r = lax.broadcasted_iota(jnp.int32, (rows, rows), 0)
    c = lax.broadcasted_iota(jnp.int32, (rows, rows), 1)
    tril = c <= r
    for g in range(SGU_GROUPS):
        sl = slice(g * SGU_GROUP_DIM, (g + 1) * SGU_GROUP_DIM)
        w = jnp.where(tril, w_ref[g], 0.0).astype(BF16)
        mixed = jnp.dot(w, v_ref[0, :, sl].astype(BF16), preferred_element_type=F32)
        o_ref[0, :, sl] = (u_ref[0, :, sl].astype(F32) * (mixed + b_ref[:, sl])).astype(o_ref.dtype)


def _sgu(u, vn, w_s, b_s, rows):
    nblk = u.shape[0]
    w = w_s[:, :rows, :rows]
    b_full = jnp.repeat(b_s[:, :rows].T, SGU_GROUP_DIM, axis=1).astype(F32)
    return pl.pallas_call(
        functools.partial(_sgu_kernel, rows=rows),
        grid=(nblk,),
        in_specs=[pl.BlockSpec((1, rows, D_SGU), lambda i: (i, 0, 0)),
                  pl.BlockSpec((1, rows, D_SGU), lambda i: (i, 0, 0)),
                  pl.BlockSpec((SGU_GROUPS, rows, rows), lambda i: (0, 0, 0)),
                  pl.BlockSpec((rows, D_SGU), lambda i: (0, 0))],
        out_specs=pl.BlockSpec((1, rows, D_SGU), lambda i: (i, 0, 0)),
        out_shape=jax.ShapeDtypeStruct((nblk, rows, D_SGU), BF16),
        compiler_params=_cparams("parallel"),
        name="sgu",
    )(u, vn, w, b_full)


def _merge_kernel(o_ref, s_ref, gate_ref, x_ref, wpa_ref, wpb_ref, wout_ref, gn_ref,
                  xo_ref, xn_ref):
    d = x_ref.shape[-1]
    ta = jnp.dot(o_ref[...], wpa_ref[...], preferred_element_type=F32)
    tb = jnp.dot(s_ref[...], wpb_ref[...], preferred_element_type=F32)
    h = gate_ref[:, :d].astype(F32) * ta + gate_ref[:, d:].astype(F32) * tb
    y = x_ref[...] + jnp.dot(h.astype(BF16), wout_ref[...], preferred_element_type=F32)
    xo_ref[...] = y
    xn_ref[...] = _rms(y, gn_ref[...]).astype(xn_ref.dtype)


def _merge(o, s, gates, x, w_pa, w_pb, w_out, g_ffn, tm):
    t, d = x.shape
    once = pl.Buffered(1)
    return pl.pallas_call(
        _merge_kernel,
        grid=(t // tm,),
        in_specs=[pl.BlockSpec((tm, D_ATTN), lambda i: (i, 0)),
                  pl.BlockSpec((tm, D_SGU), lambda i: (i, 0)),
                  pl.BlockSpec((tm, 2 * d), lambda i: (i, 0)),
                  pl.BlockSpec((tm, d), lambda i: (i, 0)),
                  pl.BlockSpec((D_ATTN, d), lambda i: (0, 0), pipeline_mode=once),
                  pl.BlockSpec((D_SGU, d), lambda i: (0, 0), pipeline_mode=once),
                  pl.BlockSpec((d, d), lambda i: (0, 0), pipeline_mode=once),
                  pl.BlockSpec((1, d), lambda i: (0, 0))],
        out_specs=[pl.BlockSpec((tm, d), lambda i: (i, 0)),
                   pl.BlockSpec((tm, d), lambda i: (i, 0))],
        out_shape=[jax.ShapeDtypeStruct((t, d), F32),
                   jax.ShapeDtypeStruct((t, d), BF16)],
        compiler_params=_cparams("parallel"),
        name="merge",
    )(o, s, gates, x, w_pa, w_pb, w_out, g_ffn.reshape(1, d))


def _peer_topk_kernel(q_ref, keys_ref, rankb_ref, e1_ref, nsel_ref, coef_ref,
                      s_ref, ranka_ref, top0_ref, top1_ref, cand_ref, cnt_ref, zinv_ref):
    nk = PEER_N_KEYS
    tb = q_ref.shape[0]
    q = q_ref[...]
    for hp in range(2 * PEER_HEADS):
        s_ref[hp * nk:(hp + 1) * nk, :] = lax.dot_general(
            keys_ref[hp], q[:, hp * PEER_HALF:(hp + 1) * PEER_HALF],
            (((1,), (1,)), ((), ())), preferred_element_type=F32)

    key_id = lax.broadcasted_iota(jnp.int32, (nk, tb), 0).astype(F32)
    for h in range(PEER_HEADS):
        rows = slice(h * nk, (h + 1) * nk)
        ranka_ref[rows, :] = jnp.full((nk, tb), float(PEER_TOPK), F32)
        rankb_ref[rows, :] = jnp.full((nk, tb), float(PEER_TOPK), F32)

        def body(r, works, h=h, rows=rows):
            rf = jnp.asarray(r, F32)
            out = []
            for w, rank_ref, top_ref in zip(works, (ranka_ref, rankb_ref), (top0_ref, top1_ref)):
                m = jnp.max(w, axis=0, keepdims=True)
                first = jnp.min(jnp.where(w == m, key_id, float(nk)), axis=0, keepdims=True)
                hit = key_id == first
                out.append(jnp.where(hit, -jnp.inf, w))
                rank_ref[rows, :] = jnp.where(hit, rf, rank_ref[rows, :])
                top_ref[r, pl.ds(h, 1), :] = m
            return tuple(out)

        lax.fori_loop(0, PEER_TOPK, body,
                      (s_ref[2 * h * nk:(2 * h + 1) * nk, :],
                       s_ref[(2 * h + 1) * nk:(2 * h + 2) * nk, :]))

    for c, (k, l) in enumerate(_CANDS):
        cand_ref[c] = top0_ref[k] + top1_ref[l]
    for k in range(PEER_TOPK):
        cnt_ref[k] = jnp.zeros(cnt_ref.shape[1:], F32)
    cv0 = top0_ref[0] + top1_ref[0]
    big = float(PEER_TOPK * PEER_TOPK)

    def body(r, z):
        m = cand_ref[0]
        for c in range(1, len(_CANDS)):
            m = jnp.maximum(m, cand_ref[c])
        idx = jnp.full(m.shape, big, F32)
        for c, (k, l) in enumerate(_CANDS):
            idx = jnp.minimum(idx, jnp.where(cand_ref[c] == m, float(k * PEER_TOPK + l), big))
        for k in range(PEER_TOPK):
            cnt = cnt_ref[k]
            for c, (kc, l) in enumerate(_CANDS):
                if kc != k:
                    continue
                hit = idx == float(k * PEER_TOPK + l)
                cand_ref[c] = jnp.where(hit, -jnp.inf, cand_ref[c])
                cnt = cnt + jnp.where(hit, 1.0, 0.0)
            cnt_ref[k] = cnt
        return z + jnp.exp(m - cv0)

    z = lax.fori_loop(0, PEER_TOPK, body, jnp.zeros(cv0.shape, F32))
    zinv_ref[...] = 1.0 / z
    for h in range(PEER_HEADS):
        rows = slice(h * nk, (h + 1) * nk)
        head = pl.ds(h, 1)
        ra = ranka_ref[rows, :]
        nsel = jnp.zeros((nk, tb), F32)
        for k in range(PEER_TOPK):
            nsel = jnp.where(ra == float(k), cnt_ref[k, head, :], nsel)
        nsel_ref[rows, :] = nsel
        s0 = s_ref[2 * h * nk:(2 * h + 1) * nk, :]
        s1 = s_ref[(2 * h + 1) * nk:(2 * h + 2) * nk, :]
        coef_ref[rows, :] = jnp.exp(s0 - top0_ref[0, head, :]) * zinv_ref[head, :]
        e1_ref[rows, :] = jnp.exp(s1 - top1_ref[0, head, :])


def _peer_topk(q, keys, tb):
    t = q.shape[0]
    rows = PEER_HEADS * PEER_N_KEYS
    out = jax.ShapeDtypeStruct((rows, t), F32)
    ospec = pl.BlockSpec((rows, tb), lambda i: (0, i))
    hv = (PEER_HEADS, tb)
    return pl.pallas_call(
        _peer_topk_kernel,
        grid=(t // tb,),
        in_specs=[pl.BlockSpec((tb, q.shape[1]), lambda i: (i, 0)),
                  pl.BlockSpec(keys.shape, lambda i: (0, 0, 0))],
        out_specs=[ospec, ospec, ospec, ospec],
        out_shape=[out, out, out, out],
        scratch_shapes=[pltpu.VMEM((2 * rows, tb), F32),
                        pltpu.VMEM((rows, tb), F32),
                        pltpu.VMEM((PEER_TOPK,) + hv, F32),
                        pltpu.VMEM((PEER_TOPK,) + hv, F32),
                        pltpu.VMEM((len(_CANDS),) + hv, F32),
                        pltpu.VMEM((PEER_TOPK,) + hv, F32),
                        pltpu.VMEM(hv, F32)],
        compiler_params=_cparams("parallel"),
        name="peer_topk",
    )(q, keys)


def _peer_expert_kernel(xt_ref, u_ref, vt_ref, rankb_ref, e1_ref, nsel_ref, coef_ref,
                        xres_ref, gf_ref, y_ref, acc_ref, a0_ref, a1_ref, h0_ref, h1_ref,
                        *, n_i, n_tiles, final_norm):
    s = pl.program_id(1)
    nk = PEER_N_KEYS
    e_tile = n_i * nk

    @pl.when(s == 0)
    def _():
        acc_ref[...] = jnp.zeros_like(acc_ref)
        a1_ref[...] = jnp.zeros_like(a1_ref)
        h0_ref[...] = jnp.zeros_like(h0_ref)
        h1_ref[...] = jnp.zeros_like(h1_ref)

    def stage_a(half, a_dst):
        a_dst[...] = jnp.dot(u_ref[half * e_tile:(half + 1) * e_tile, :], xt_ref[...],
                             preferred_element_type=F32)

    def stage_b(tile, a_src, h_dst):
        tile = jnp.clip(tile, 0, n_tiles - 1)
        for il in range(n_i):
            i = tile * n_i + il
            gate = None
            for h in range(PEER_HEADS):
                hs = slice(h * nk, (h + 1) * nk)
                nsel_row = nsel_ref[pl.ds(h * nk + i, 1), :]
                coef_row = coef_ref[pl.ds(h * nk + i, 1), :]
                sel = jnp.where(rankb_ref[hs, :] < nsel_row, e1_ref[hs, :], 0.0) * coef_row
                gate = sel if gate is None else gate + sel
            rows = slice(il * nk, (il + 1) * nk)
            h_dst[rows, :] = (gate * jax.nn.gelu(a_src[rows, :])).astype(h_dst.dtype)

    def stage_c(half, h_src):
        acc_ref[...] += jnp.dot(vt_ref[:, half * e_tile:(half + 1) * e_tile], h_src[...],
                                preferred_element_type=F32)

    stage_c(0, h0_ref)
    stage_b(2 * s - 1, a1_ref, h1_ref)
    stage_a(0, a0_ref)

    stage_c(1, h1_ref)
    stage_b(2 * s, a0_ref, h0_ref)
    stage_a(1, a1_ref)

    @pl.when(s == pl.num_programs(1) - 1)
    def _():
        y = xres_ref[...] + acc_ref[...].T
        if final_norm:
            y = _rms(y, gf_ref[...])
        y_ref[...] = y


def _peer_experts(xn_t, u, v_t, rankb, e1, nsel, coef, xres, g_final, tb, e_tile, final_norm):
    d, t = xn_t.shape
    n_tiles = u.shape[0] // e_tile
    n_steps = n_tiles // 2 + 1
    rows = PEER_HEADS * PEER_N_KEYS
    aux = pl.BlockSpec((rows, tb), lambda i, j: (0, i))
    return pl.pallas_call(
        functools.partial(_peer_expert_kernel, n_i=e_tile // PEER_N_KEYS, n_tiles=n_tiles,
                          final_norm=final_norm),
        grid=(t // tb, n_steps),
        in_specs=[pl.BlockSpec((d, tb), lambda i, j: (0, i)),
                  pl.BlockSpec((2 * e_tile, d), lambda i, j: (jnp.minimum(j, n_steps - 2), 0)),
                  pl.BlockSpec((d, 2 * e_tile), lambda i, j: (0, jnp.maximum(j - 1, 0))),
                  aux, aux, aux, aux,
                  pl.BlockSpec((tb, d), lambda i, j: (i, 0), pipeline_mode=pl.Buffered(1)),
                  pl.BlockSpec((1, d), lambda i, j: (0, 0))],
        out_specs=pl.BlockSpec((tb, d), lambda i, j: (i, 0)),
        out_shape=jax.ShapeDtypeStruct((t, d), F32),
        scratch_shapes=[pltpu.VMEM((d, tb), F32),
                        pltpu.VMEM((e_tile, tb), F32),
                        pltpu.VMEM((e_tile, tb), F32),
                        pltpu.VMEM((e_tile, tb), BF16),
                        pltpu.VMEM((e_tile, tb), BF16)],
        compiler_params=_cparams("parallel", "arbitrary"),
        name="peer_experts",
    )(xn_t, u, v_t, rankb, e1, nsel, coef, xres, g_final.reshape(1, d))


def _row_tile(t, pref):
    return pref if t % pref == 0 else t


def _in_projection(x2, g_mix, w_in, sgu_g):
    t = x2.shape[0]
    xn = _norm_cast(x2, g_mix, _row_tile(t, 512))
    tm = _row_tile(t, 1024)
    c0, c1, c2, c3, c4 = D_ATTN, D_ATTN + 2 * D_KV, D_ATTN + 2 * D_KV + D_SGU, \
        D_ATTN + 2 * D_KV + 2 * D_SGU, w_in.shape[1]
    q = _mm(xn, w_in[:, :c0], "scale_q", BF16, tm, 1024, name="proj_q")
    kv = _mm(xn, w_in[:, c0:c1], "none", F32, tm, 2 * D_KV, name="proj_kv")
    u = _mm(xn, w_in[:, c1:c2], "gelu", BF16, tm, 1024, name="proj_u")
    vs = _mm(xn, w_in[:, c2:c3], "gelu_norm", F32, _row_tile(t, 512), D_SGU, g=sgu_g, name="proj_vsgu")
    gates = _mm(xn, w_in[:, c3:c4], "sigmoid", BF16, tm, 1024, name="proj_gates")
    return q, kv, u, vs, gates


def _peer(xn, xres, w_query, keys, u, v_t, g_final, final_norm):
    t = xn.shape[0]
    q = _mm(xn, w_query, "none", BF16, _row_tile(t, 1024), 1024, name="peer_query")
    rankb, e1, nsel, coef = _peer_topk(q, keys, LANES)
    tb = _row_tile(t, 512)
    return _peer_experts(xn.T, u, v_t, rankb, e1, nsel, coef, xres, g_final, tb, 256, final_norm)


def kernel(x_prompt, x_sample, cache_k_swa, cache_v_swa, norm_mix_g, w_in, sgu_norm_g, sgu_w_s,
           sgu_b_s, attn_sinks, rel_bias_table, w_branch_attn, w_branch_sgu, w_out, norm_ffn_g,
           peer_w_query, peer_sub_keys, peer_expert_u, peer_expert_v, norm_final_g):
    B, S, D = x_prompt.shape
    Bd, T, _ = x_sample.shape
    depth = w_in.shape[0]
    bias_p = _rel_bias(rel_bias_table, CHUNK, WINDOW + CHUNK)
    bias_s = _rel_bias(rel_bias_table, T, WINDOW + T)

    xp = x_prompt.reshape(B * S, D)
    xs = x_sample.reshape(Bd * T, D)
    nk_p, nv_p, nk_s, nv_s, nsgu_s = [], [], [], [], []
    for l in range(depth):
        last = l == depth - 1
        w_in_l = w_in[l].astype(BF16)
        w_pa = w_branch_attn[l].astype(BF16)
        w_pb = w_branch_sgu[l].astype(BF16)
        w_o = w_out[l].astype(BF16)
        w_q = peer_w_query[l].astype(BF16)
        keys = peer_sub_keys[l].astype(BF16).reshape(2 * PEER_HEADS, PEER_N_KEYS, PEER_HALF)
        e_u = peer_expert_u[l].astype(BF16)
        e_vt = peer_expert_v[l].astype(BF16).T
        g_out = norm_final_g if last else jnp.ones_like(norm_final_g)

        q, kv, u, vs, gates = _in_projection(xp, norm_mix_g[l], w_in_l, sgu_norm_g[l])
        kv3 = kv.reshape(B, S, 2 * D_KV)
        kpad = jnp.pad(kv3[..., :D_KV].astype(BF16), ((0, 0), (WINDOW, 0), (0, 0)))
        vpad = jnp.pad(kv3[..., D_KV:].astype(BF16), ((0, 0), (WINDOW, 0), (0, 0)))
        o = _attention(q.reshape(B, S, D_ATTN), kpad, vpad, bias_p, attn_sinks[l],
                       CHUNK, WINDOW + CHUNK, WINDOW)
        n_blk = S // SGU_BLOCK
        sg = _sgu(u.reshape(B * n_blk, SGU_BLOCK, D_SGU), vs.reshape(B * n_blk, SGU_BLOCK, D_SGU),
                  sgu_w_s[l], sgu_b_s[l], SGU_BLOCK)
        xp, xnp = _merge(o.reshape(B * S, D_ATTN), sg.reshape(B * S, D_SGU), gates, xp,
                         w_pa, w_pb, w_o, norm_ffn_g[l], _row_tile(B * S, 256))
        nk_p.append(kv3[:, S - WINDOW:, :D_KV].reshape(B, WINDOW, N_KV_HEADS, HEAD_DIM))
        nv_p.append(kv3[:, S - WINDOW:, D_KV:].reshape(B, WINDOW, N_KV_HEADS, HEAD_DIM))

        qs, kvs, us, vss, gates_s = _in_projection(xs, norm_mix_g[l], w_in_l, sgu_norm_g[l])
        kvs3 = kvs.reshape(Bd, T, 2 * D_KV)
        k_all = jnp.concatenate([cache_k_swa[l].reshape(Bd, WINDOW, D_KV).astype(BF16),
                                 kvs3[..., :D_KV].astype(BF16)], axis=1)
        v_all = jnp.concatenate([cache_v_swa[l].reshape(Bd, WINDOW, D_KV).astype(BF16),
                                 kvs3[..., D_KV:].astype(BF16)], axis=1)
        os_ = _attention(qs.reshape(Bd, T, D_ATTN), k_all, v_all, bias_s, attn_sinks[l],
                         T, WINDOW + T, 0)
        ss = _sgu(us.reshape(Bd, T, D_SGU), vss.reshape(Bd, T, D_SGU), sgu_w_s[l], sgu_b_s[l], T)
        xs, xns = _merge(os_.reshape(Bd * T, D_ATTN), ss.reshape(Bd * T, D_SGU), gates_s, xs,
                         w_pa, w_pb, w_o, norm_ffn_g[l], Bd * T)
        nk_s.append(kvs3[..., :D_KV].reshape(Bd, T, N_KV_HEADS, HEAD_DIM))
        nv_s.append(kvs3[..., D_KV:].reshape(Bd, T, N_KV_HEADS, HEAD_DIM))
        nsgu_s.append(vss.reshape(Bd, T, D_SGU))

        xp = _peer(xnp, xp, w_q, keys, e_u, e_vt, g_out, last)
        xs = _peer(xns, xs, w_q, keys, e_u, e_vt, g_out, last)

    return (xp.reshape(B, S, D), xs.reshape(Bd, T, D), jnp.stack(nk_p), jnp.stack(nv_p),
            jnp.stack(nk_s), jnp.stack(nv_s), jnp.stack(nsgu_s))
```

```python
import functools
import math

import jax
import jax.numpy as jnp
import numpy as np
from jax import lax
from jax.experimental import pallas as pl
from jax.experimental.pallas import tpu as pltpu

F32 = jnp.float32
BF16 = jnp.bfloat16

CHUNK = 64
WINDOW = 128
N_HEADS = 16
N_KV_HEADS = 2
HEAD_DIM = 64
GQA_GROUP = N_HEADS // N_KV_HEADS
D_ATTN = N_HEADS * HEAD_DIM
D_KV = N_KV_HEADS * HEAD_DIM
N_BUCKETS = 32
MAX_DISTANCE = 128
SGU_BLOCK = 128
SGU_GROUPS = 8
SGU_GROUP_DIM = 128
D_SGU = SGU_GROUPS * SGU_GROUP_DIM
PEER_HEADS = 8
PEER_N_KEYS = 128
PEER_HALF = 128
PEER_TOPK = 16
EPS = 1e-6
NEG_INF = -1e30

LANES = 128
SUBLANES = 8
VMEM_LIMIT_BYTES = 56 * 1024 * 1024

_CANDS = tuple((k, l) for k in range(PEER_TOPK) for l in range(PEER_TOPK)
               if (k + 1) * (l + 1) <= PEER_TOPK)


def _cparams(*sem):
    return pltpu.CompilerParams(dimension_semantics=sem, vmem_limit_bytes=VMEM_LIMIT_BYTES)


def _rms(x, g):
    r = lax.rsqrt(jnp.mean(x * x, axis=-1, keepdims=True) + EPS)
    return x * r * g


def _norm_kernel(x_ref, g_ref, o_ref):
    o_ref[...] = _rms(x_ref[...], g_ref[...]).astype(o_ref.dtype)


def _norm_cast(x, g, tm):
    t, d = x.shape
    return pl.pallas_call(
        _norm_kernel,
        grid=(t // tm,),
        in_specs=[pl.BlockSpec((tm, d), lambda i: (i, 0)),
                  pl.BlockSpec((1, d), lambda i: (0, 0))],
        out_specs=pl.BlockSpec((tm, d), lambda i: (i, 0)),
        out_shape=jax.ShapeDtypeStruct((t, d), BF16),
        compiler_params=_cparams("parallel"),
        name="rmsnorm_cast",
    )(x, g.reshape(1, d))


def _mm_kernel(x_ref, w_ref, *rest, epilogue):
    acc = jnp.dot(x_ref[...], w_ref[...], preferred_element_type=F32)
    if epilogue == "scale_q":
        (o_ref,) = rest
        o_ref[...] = (acc * (HEAD_DIM ** -0.5)).astype(o_ref.dtype)
    elif epilogue == "none":
        (o_ref,) = rest
        o_ref[...] = acc.astype(o_ref.dtype)
    elif epilogue == "gelu":
        (o_ref,) = rest
        o_ref[...] = jax.nn.gelu(acc).astype(o_ref.dtype)
    elif epilogue == "gelu_norm":
        g_ref, o_ref = rest
        o_ref[...] = _rms(jax.nn.gelu(acc), g_ref[...]).astype(o_ref.dtype)
    elif epilogue == "sigmoid":
        (o_ref,) = rest
        o_ref[...] = jax.nn.sigmoid(acc).astype(o_ref.dtype)
    else:
        raise ValueError(epilogue)


def _mm(x, w, epilogue, out_dtype, tm, tn, g=None, name="mm"):
    t, k = x.shape
    n = w.shape[1]
    in_specs = [pl.BlockSpec((tm, k), lambda i, j: (i, 0)),
                pl.BlockSpec((k, tn), lambda i, j: (0, j))]
    args = [x, w]
    if g is not None:
        in_specs.append(pl.BlockSpec((1, tn), lambda i, j: (0, j)))
        args.append(g.reshape(1, n))
    return pl.pallas_call(
        functools.partial(_mm_kernel, epilogue=epilogue),
        grid=(t // tm, n // tn),
        in_specs=in_specs,
        out_specs=pl.BlockSpec((tm, tn), lambda i, j: (i, j)),
        out_shape=jax.ShapeDtypeStruct((t, n), out_dtype),
        compiler_params=_cparams("parallel", "parallel"),
        name=name,
    )(*args)


def _bias_kernel(table_ref, bucket_ref, o_ref):
    bucket = bucket_ref[...]
    for h in range(N_HEADS):
        acc = jnp.zeros(bucket.shape, F32)
        for b in range(N_BUCKETS):
            acc = jnp.where(bucket == b, table_ref[b, h], acc)
        o_ref[h] = acc


def _t5_bucket(rel):
    half = N_BUCKETS // 2
    max_exact = half // 2
    ret = jnp.where(rel > 0, half, 0)
    n = jnp.abs(rel)
    nf = jnp.maximum(n, 1).astype(jnp.float32)
    large = max_exact + (jnp.log(nf / max_exact) / math.log(MAX_DISTANCE / max_exact)
                         * (half - max_exact)).astype(jnp.int32)
    large = jnp.minimum(large, half - 1)
    return ret + jnp.where(n < max_exact, n, large)


def _rel_bias(table, n_q, n_k):
    rel = (jnp.arange(n_k, dtype=jnp.int32)[None, :] - WINDOW
           - jnp.arange(n_q, dtype=jnp.int32)[:, None])
    bucket = _t5_bucket(rel).astype(jnp.int32)
    return pl.pallas_call(
        _bias_kernel,
        in_specs=[pl.BlockSpec(memory_space=pltpu.SMEM),
                  pl.BlockSpec(memory_space=pltpu.VMEM)],
        out_specs=pl.BlockSpec(memory_space=pltpu.VMEM),
        out_shape=jax.ShapeDtypeStruct((N_HEADS, n_q, n_k), F32),
        name="rel_bias",
    )(table.astype(F32), bucket)


def _attn_kernel(sinks_ref, q_ref, k_ref, v_ref, bias_ref, o_ref, *, n_q, n_k, n_pad):
    c = pl.program_id(1)
    start = pl.multiple_of(c * n_q, n_q)
    kb = k_ref[0, pl.ds(start, n_k), :]
    vb = v_ref[0, pl.ds(start, n_k), :]
    key_row = start + lax.broadcasted_iota(jnp.int32, (1, n_k), 1)
    valid = key_row >= n_pad
    heads = range(N_HEADS)
    kv_cols = [slice((h // GQA_GROUP) * HEAD_DIM, (h // GQA_GROUP + 1) * HEAD_DIM) for h in heads]
    s = [lax.dot_general(q_ref[0, :, h * HEAD_DIM:(h + 1) * HEAD_DIM], kb[:, kv_cols[h]],
                         (((1,), (1,)), ((), ())), preferred_element_type=F32) for h in heads]
    s = [jnp.where(valid, s[h] + bias_ref[h], NEG_INF) for h in heads]
    m = [jnp.maximum(jnp.max(s[h], axis=-1, keepdims=True), sinks_ref[h]) for h in heads]
    p = [jnp.exp(s[h] - m[h]) for h in heads]
    denom = [jnp.sum(p[h], axis=-1, keepdims=True) + jnp.exp(sinks_ref[h] - m[h]) for h in heads]
    o = [jnp.dot(p[h].astype(BF16), vb[:, kv_cols[h]], preferred_element_type=F32) for h in heads]
    for h in heads:
        o_ref[0, :, h * HEAD_DIM:(h + 1) * HEAD_DIM] = (o[h] / denom[h]).astype(o_ref.dtype)


def _attention(q, k, v, bias, sinks, n_q, n_k, n_pad):
    nb, sq, _ = q.shape
    rows = k.shape[1]
    return pl.pallas_call(
        functools.partial(_attn_kernel, n_q=n_q, n_k=n_k, n_pad=n_pad),
        grid=(nb, sq // n_q),
        in_specs=[pl.BlockSpec(memory_space=pltpu.SMEM),
                  pl.BlockSpec((1, n_q, D_ATTN), lambda b, c: (b, c, 0)),
                  pl.BlockSpec((1, rows, D_KV), lambda b, c: (b, 0, 0)),
                  pl.BlockSpec((1, rows, D_KV), lambda b, c: (b, 0, 0)),
                  pl.BlockSpec((N_HEADS, n_q, n_k), lambda b, c: (0, 0, 0))],
        out_specs=pl.BlockSpec((1, n_q, D_ATTN), lambda b, c: (b, c, 0)),
        out_shape=jax.ShapeDtypeStruct((nb, sq, D_ATTN), BF16),
        compiler_params=_cparams("parallel", "arbitrary"),
        name="swa_attention",
    )(sinks.astype(F32), q, k, v, bias)


def _sgu_kernel(u_ref, v_ref, w_ref, b_ref, o_ref, *, rows):
    r = lax.broadcasted_iota(jnp.int32, (rows, rows), 0)
    c = lax.broadcasted_iota(jnp.int32, (rows, rows), 1)
    tril = c <= r
    for g in range(SGU_GROUPS):
        sl = slice(g * SGU_GROUP_DIM, (g + 1) * SGU_GROUP_DIM)
        w = jnp.where(tril, w_ref[g], 0.0).astype(BF16)
        mixed = jnp.dot(w, v_ref[0, :, sl].astype(BF16), preferred_element_type=F32)
        o_ref[0, :, sl] = (u_ref[0, :, sl].astype(F32) * (mixed + b_ref[:, sl])).astype(o_ref.dtype)


def _sgu(u, vn, w_s, b_s, rows):
    nblk = u.shape[0]
    w = w_s[:, :rows, :rows]
    b_full = jnp.repeat(b_s[:, :rows].T, SGU_GROUP_DIM, axis=1).astype(F32)
    return pl.pallas_call(
        functools.partial(_sgu_kernel, rows=rows),
        grid=(nblk,),
        in_specs=[pl.BlockSpec((1, rows, D_SGU), lambda i: (i, 0, 0)),
                  pl.BlockSpec((1, rows, D_SGU), lambda i: (i, 0, 0)),
                  pl.BlockSpec((SGU_GROUPS, rows, rows), lambda i: (0, 0, 0)),
                  pl.BlockSpec((rows, D_SGU), lambda i: (0, 0))],
        out_specs=pl.BlockSpec((1, rows, D_SGU), lambda i: (i, 0, 0)),
        out_shape=jax.ShapeDtypeStruct((nblk, rows, D_SGU), BF16),
        compiler_params=_cparams("parallel"),
        name="sgu",
    )(u, vn, w, b_full)


def _merge_kernel(o_ref, s_ref, gate_ref, x_ref, wpa_ref, wpb_ref, wout_ref, gn_ref,
                  xo_ref, xn_ref):
    d = x_ref.shape[-1]
    ta = jnp.dot(o_ref[...], wpa_ref[...], preferred_element_type=F32)
    tb = jnp.dot(s_ref[...], wpb_ref[...], preferred_element_type=F32)
    h = gate_ref[:, :d].astype(F32) * ta + gate_ref[:, d:].astype(F32) * tb
    y = x_ref[...] + jnp.dot(h.astype(BF16), wout_ref[...], preferred_element_type=F32)
    xo_ref[...] = y
    xn_ref[...] = _rms(y, gn_ref[...]).astype(xn_ref.dtype)


def _merge(o, s, gates, x, w_pa, w_pb, w_out, g_ffn, tm):
    t, d = x.shape
    once = pl.Buffered(1)
    return pl.pallas_call(
        _merge_kernel,
        grid=(t // tm,),
        in_specs=[pl.BlockSpec((tm, D_ATTN), lambda i: (i, 0)),
                  pl.BlockSpec((tm, D_SGU), lambda i: (i, 0)),
                  pl.BlockSpec((tm, 2 * d), lambda i: (i, 0)),
                  pl.BlockSpec((tm, d), lambda i: (i, 0)),
                  pl.BlockSpec((D_ATTN, d), lambda i: (0, 0), pipeline_mode=once),
                  pl.BlockSpec((D_SGU, d), lambda i: (0, 0), pipeline_mode=once),
                  pl.BlockSpec((d, d), lambda i: (0, 0), pipeline_mode=once),
                  pl.BlockSpec((1, d), lambda i: (0, 0))],
        out_specs=[pl.BlockSpec((tm, d), lambda i: (i, 0)),
                   pl.BlockSpec((tm, d), lambda i: (i, 0))],
        out_shape=[jax.ShapeDtypeStruct((t, d), F32),
                   jax.ShapeDtypeStruct((t, d), BF16)],
        compiler_params=_cparams("parallel"),
        name="merge",
    )(o, s, gates, x, w_pa, w_pb, w_out, g_ffn.reshape(1, d))


def _peer_topk_kernel(q_ref, keys_ref, rankb_out_ref, e1_ref, nsel_ref, coef_ref,
                      s_ref, ranka_ref, rankb_ref, top0_ref, top1_ref, cand_ref, cnt_ref, zinv_ref):
    nk = PEER_N_KEYS
    tb = q_ref.shape[0]
    q = q_ref[...]
    for hp in range(2 * PEER_HEADS):
        s_ref[hp * nk:(hp + 1) * nk, :] = lax.dot_general(
            keys_ref[hp], q[:, hp * PEER_HALF:(hp + 1) * PEER_HALF],
            (((1,), (1,)), ((), ())), preferred_element_type=F32)

    key_id = lax.broadcasted_iota(jnp.int32, (nk, tb), 0).astype(F32)
    for h in range(PEER_HEADS):
        rows = slice(h * nk, (h + 1) * nk)
        ranka_ref[rows, :] = jnp.full((nk, tb), float(PEER_TOPK), F32)
        rankb_ref[rows, :] = jnp.full((nk, tb), float(PEER_TOPK), F32)

        def body(r, works, h=h, rows=rows):
            rf = jnp.asarray(r, F32)
            out = []
            for w, rank_ref, top_ref in zip(works, (ranka_ref, rankb_ref), (top0_ref, top1_ref)):
                m = jnp.max(w, axis=0, keepdims=True)
                first = jnp.min(jnp.where(w == m, key_id, float(nk)), axis=0, keepdims=True)
                hit = key_id == first
                out.append(jnp.where(hit, -jnp.inf, w))
                rank_ref[rows, :] = jnp.where(hit, rf, rank_ref[rows, :])
                top_ref[r, pl.ds(h, 1), :] = m
            return tuple(out)

        lax.fori_loop(0, PEER_TOPK, body,
                      (s_ref[2 * h * nk:(2 * h + 1) * nk, :],
                       s_ref[(2 * h + 1) * nk:(2 * h + 2) * nk, :]))

    for c, (k, l) in enumerate(_CANDS):
        cand_ref[c] = top0_ref[k] + top1_ref[l]
    for k in range(PEER_TOPK):
        cnt_ref[k] = jnp.zeros(cnt_ref.shape[1:], F32)
    cv0 = top0_ref[0] + top1_ref[0]
    big = float(PEER_TOPK * PEER_TOPK)

    def body(r, z):
        m = cand_ref[0]
        for c in range(1, len(_CANDS)):
            m = jnp.maximum(m, cand_ref[c])
        idx = jnp.full(m.shape, big, F32)
        for c, (k, l) in enumerate(_CANDS):
            idx = jnp.minimum(idx, jnp.where(cand_ref[c] == m, float(k * PEER_TOPK + l), big))
        for k in range(PEER_TOPK):
            cnt = cnt_ref[k]
            for c, (kc, l) in enumerate(_CANDS):
                if kc != k:
                    continue
                hit = idx == float(k * PEER_TOPK + l)
                cand_ref[c] = jnp.where(hit, -jnp.inf, cand_ref[c])
                cnt = cnt + jnp.where(hit, 1.0, 0.0)
            cnt_ref[k] = cnt
        return z + jnp.exp(m - cv0)

    z = lax.fori_loop(0, PEER_TOPK, body, jnp.zeros(cv0.shape, F32))
    zinv_ref[...] = 1.0 / z
    for h in range(PEER_HEADS):
        rows = slice(h * nk, (h + 1) * nk)
        head = pl.ds(h, 1)
        ra = ranka_ref[rows, :]
        nsel = jnp.zeros((nk, tb), F32)
        for k in range(PEER_TOPK):
            nsel = jnp.where(ra == float(k), cnt_ref[k, head, :], nsel)
        nsel_ref[rows, :] = nsel
        s0 = s_ref[2 * h * nk:(2 * h + 1) * nk, :]
        s1 = s_ref[(2 * h + 1) * nk:(2 * h + 2) * nk, :]
        coef_ref[rows, :] = jnp.exp(s0 - top0_ref[0, head, :]) * zinv_ref[head, :]
        e1_ref[rows, :] = jnp.exp(s1 - top1_ref[0, head, :]).astype(e1_ref.dtype)
        rankb_out_ref[rows, :] = rankb_ref[rows, :].astype(rankb_out_ref.dtype)


def _peer_topk(q, keys, tb):
    t = q.shape[0]
    rows = PEER_HEADS * PEER_N_KEYS
    out = jax.ShapeDtypeStruct((rows, t), F32)
    out_b = jax.ShapeDtypeStruct((rows, t), BF16)
    ospec = pl.BlockSpec((rows, tb), lambda i: (0, i))
    hv = (PEER_HEADS, tb)
    return pl.pallas_call(
        _peer_topk_kernel,
        grid=(t // tb,),
        in_specs=[pl.BlockSpec((tb, q.shape[1]), lambda i: (i, 0)),
                  pl.BlockSpec(keys.shape, lambda i: (0, 0, 0))],
        out_specs=[ospec, ospec, ospec, ospec],
        out_shape=[out_b, out_b, out, out],
        scratch_shapes=[pltpu.VMEM((2 * rows, tb), F32),
                        pltpu.VMEM((rows, tb), F32),
                        pltpu.VMEM((rows, tb), F32),
                        pltpu.VMEM((PEER_TOPK,) + hv, F32),
                        pltpu.VMEM((PEER_TOPK,) + hv, F32),
                        pltpu.VMEM((len(_CANDS),) + hv, F32),
                        pltpu.VMEM((PEER_TOPK,) + hv, F32),
                        pltpu.VMEM(hv, F32)],
        compiler_params=_cparams("parallel"),
        name="peer_topk",
    )(q, keys)


def _peer_expert_kernel(xt_ref, u_ref, vt_ref, rankb_ref, e1_ref, nsel_ref, coef_ref,
                        xres_ref, gf_ref, y_ref, acc_ref, *, n_chunks, final_norm):
    et = pl.program_id(1)
    nk = PEER_N_KEYS
    i_per_chunk = 2
    ck = i_per_chunk * nk

    @pl.when(et == 0)
    def _():
        acc_ref[...] = jnp.zeros_like(acc_ref)

    def up_proj(c):
        return jnp.dot(u_ref[c * ck:(c + 1) * ck, :], xt_ref[...],
                       preferred_element_type=F32)

    ahead = 2
    a_next = [up_proj(c) for c in range(min(ahead, n_chunks))]
    zero = jnp.zeros((), BF16)
    for c in range(n_chunks):
        rows = slice(c * ck, (c + 1) * ck)
        if c + ahead < n_chunks:
            a_next.append(up_proj(c + ahead))
        a_t = a_next[c]
        hidden = []
        for ic in range(i_per_chunk):
            i = (et * n_chunks + c) * i_per_chunk + ic
            gate = None
            for h in range(PEER_HEADS):
                hs = slice(h * nk, (h + 1) * nk)
                nsel_row = nsel_ref[pl.ds(h * nk + i, 1), :].astype(BF16)
                coef_row = coef_ref[pl.ds(h * nk + i, 1), :].astype(BF16)
                shape = rankb_ref[hs, :].shape
                sel = jnp.where(rankb_ref[hs, :] < jnp.broadcast_to(nsel_row, shape),
                                e1_ref[hs, :], zero) * jnp.broadcast_to(coef_row, shape)
                gate = sel if gate is None else gate + sel
            act = jax.nn.gelu(a_t[ic * nk:(ic + 1) * nk, :]).astype(BF16)
            hidden.append(gate * act)
        acc_ref[...] += jnp.dot(vt_ref[:, rows], jnp.concatenate(hidden, axis=0),
                                preferred_element_type=F32)

    @pl.when(et == pl.num_programs(1) - 1)
    def _():
        y = xres_ref[...] + acc_ref[...].T
        if final_norm:
            y = _rms(y, gf_ref[...])
        y_ref[...] = y


def _peer_experts(xn_t, u, v_t, rankb, e1, nsel, coef, xres, g_final, tb, e_tile, final_norm):
    d, t = xn_t.shape
    rows = PEER_HEADS * PEER_N_KEYS
    aux = pl.BlockSpec((rows, tb), lambda i, j: (0, i))
    return pl.pallas_call(
        functools.partial(_peer_expert_kernel, n_chunks=e_tile // (2 * PEER_N_KEYS),
                          final_norm=final_norm),
        grid=(t // tb, u.shape[0] // e_tile),
        in_specs=[pl.BlockSpec((d, tb), lambda i, j: (0, i)),
                  pl.BlockSpec((e_tile, d), lambda i, j: (j, 0)),
                  pl.BlockSpec((d, e_tile), lambda i, j: (0, j)),
                  aux, aux, aux, aux,
                  pl.BlockSpec((tb, d), lambda i, j: (i, 0), pipeline_mode=pl.Buffered(1)),
                  pl.BlockSpec((1, d), lambda i, j: (0, 0))],
        out_specs=pl.BlockSpec((tb, d), lambda i, j: (i, 0)),
        out_shape=jax.ShapeDtypeStruct((t, d), F32),
        scratch_shapes=[pltpu.VMEM((d, tb), F32)],
        compiler_params=_cparams("parallel", "arbitrary"),
        name="peer_experts",
    )(xn_t, u, v_t, rankb, e1, nsel, coef, xres, g_final.reshape(1, d))


def _row_tile(t, pref):
    return pref if t % pref == 0 else t


def _in_projection(x2, g_mix, w_in, sgu_g):
    t = x2.shape[0]
    xn = _norm_cast(x2, g_mix, _row_tile(t, 512))
    tm = _row_tile(t, 1024)
    c0, c1, c2, c3, c4 = D_ATTN, D_ATTN + 2 * D_KV, D_ATTN + 2 * D_KV + D_SGU, \
        D_ATTN + 2 * D_KV + 2 * D_SGU, w_in.shape[1]
    q = _mm(xn, w_in[:, :c0], "scale_q", BF16, tm, 1024, name="proj_q")
    kv = _mm(xn, w_in[:, c0:c1], "none", F32, tm, 2 * D_KV, name="proj_kv")
    u = _mm(xn, w_in[:, c1:c2], "gelu", BF16, tm, 1024, name="proj_u")
    vs = _mm(xn, w_in[:, c2:c3], "gelu_norm", F32, _row_tile(t, 512), D_SGU, g=sgu_g, name="proj_vsgu")
    gates = _mm(xn, w_in[:, c3:c4], "sigmoid", BF16, tm, 1024, name="proj_gates")
    return q, kv, u, vs, gates


def _peer(xn, xres, w_query, keys, u, v_t, g_final, final_norm):
    t = xn.shape[0]
    q = _mm(xn, w_query, "none", BF16, _row_tile(t, 1024), 1024, name="peer_query")
    rankb, e1, nsel, coef = _peer_topk(q, keys, LANES)
    tb = _row_tile(t, 512)
    return _peer_experts(xn.T, u, v_t, rankb, e1, nsel, coef, xres, g_final, tb, 1024, final_norm)


def kernel(x_prompt, x_sample, cache_k_swa, cache_v_swa, norm_mix_g, w_in, sgu_norm_g, sgu_w_s,
           sgu_b_s, attn_sinks, rel_bias_table, w_branch_attn, w_branch_sgu, w_out, norm_ffn_g,
           peer_w_query, peer_sub_keys, peer_expert_u, peer_expert_v, norm_final_g):
    B, S, D = x_prompt.shape
    Bd, T, _ = x_sample.shape
    depth = w_in.shape[0]
    bias_p = _rel_bias(rel_bias_table, CHUNK, WINDOW + CHUNK)
    bias_s = _rel_bias(rel_bias_table, T, WINDOW + T)

    xp = x_prompt.reshape(B * S, D)
    xs = x_sample.reshape(Bd * T, D)
    nk_p, nv_p, nk_s, nv_s, nsgu_s = [], [], [], [], []
    for l in range(depth):
        last = l == depth - 1
        w_in_l = w_in[l].astype(BF16)
        w_pa = w_branch_attn[l].astype(BF16)
        w_pb = w_branch_sgu[l].astype(BF16)
        w_o = w_out[l].astype(BF16)
        w_q = peer_w_query[l].astype(BF16)
        keys = peer_sub_keys[l].astype(BF16).reshape(2 * PEER_HEADS, PEER_N_KEYS, PEER_HALF)
        e_u = peer_expert_u[l].astype(BF16)
        e_vt = peer_expert_v[l].astype(BF16).T
        g_out = norm_final_g if last else jnp.ones_like(norm_final_g)

        q, kv, u, vs, gates = _in_projection(xp, norm_mix_g[l], w_in_l, sgu_norm_g[l])
        kv3 = kv.reshape(B, S, 2 * D_KV)
        kpad = jnp.pad(kv3[..., :D_KV].astype(BF16), ((0, 0), (WINDOW, 0), (0, 0)))
        vpad = jnp.pad(kv3[..., D_KV:].astype(BF16), ((0, 0), (WINDOW, 0), (0, 0)))
        o = _attention(q.reshape(B, S, D_ATTN), kpad, vpad, bias_p, attn_sinks[l],
                       CHUNK, WINDOW + CHUNK, WINDOW)
        n_blk = S // SGU_BLOCK
        sg = _sgu(u.reshape(B * n_blk, SGU_BLOCK, D_SGU), vs.reshape(B * n_blk, SGU_BLOCK, D_SGU),
                  sgu_w_s[l], sgu_b_s[l], SGU_BLOCK)
        xp, xnp = _merge(o.reshape(B * S, D_ATTN), sg.reshape(B * S, D_SGU), gates, xp,
                         w_pa, w_pb, w_o, norm_ffn_g[l], _row_tile(B * S, 256))
        nk_p.append(kv3[:, S - WINDOW:, :D_KV].reshape(B, WINDOW, N_KV_HEADS, HEAD_DIM))
        nv_p.append(kv3[:, S - WINDOW:, D_KV:].reshape(B, WINDOW, N_KV_HEADS, HEAD_DIM))

        qs, kvs, us, vss, gates_s = _in_projection(xs, norm_mix_g[l], w_in_l, sgu_norm_g[l])
        kvs3 = kvs.reshape(Bd, T, 2 * D_KV)
        k_all = jnp.concatenate([cache_k_swa[l].reshape(Bd, WINDOW, D_KV).astype(BF16),
                                 kvs3[..., :D_KV].astype(BF16)], axis=1)
        v_all = jnp.concatenate([cache_v_swa[l].reshape(Bd, WINDOW, D_KV).astype(BF16),
                                 kvs3[..., D_KV:].astype(BF16)], axis=1)
        os_ = _attention(qs.reshape(Bd, T, D_ATTN), k_all, v_all, bias_s, attn_sinks[l],
                         T, WINDOW + T, 0)
        ss = _sgu(us.reshape(Bd, T, D_SGU), vss.reshape(Bd, T, D_SGU), sgu_w_s[l], sgu_b_s[l], T)
        xs, xns = _merge(os_.reshape(Bd * T, D_ATTN), ss.reshape(Bd * T, D_SGU), gates_s, xs,
                         w_pa, w_pb, w_o, norm_ffn_g[l], Bd * T)
        nk_s.append(kvs3[..., :D_KV].reshape(Bd, T, N_KV_HEADS, HEAD_DIM))
        nv_s.append(kvs3[..., D_KV:].reshape(Bd, T, N_KV_HEADS, HEAD_DIM))
        nsgu_s.append(vss.reshape(Bd, T, D_SGU))

        xp = _peer(xnp, xp, w_q, keys, e_u, e_vt, g_out, last)
        xs = _peer(xns, xs, w_q, keys, e_u, e_vt, g_out, last)

    return (xp.reshape(B, S, D), xs.reshape(Bd, T, D), jnp.stack(nk_p), jnp.stack(nv_p),
            jnp.stack(nk_s), jnp.stack(nv_s), jnp.stack(nsgu_s))
```

```python
import functools
import math

import jax
import jax.numpy as jnp
import numpy as np
from jax import lax
from jax.experimental import pallas as pl
from jax.experimental.pallas import tpu as pltpu

F32 = jnp.float32
BF16 = jnp.bfloat16

CHUNK = 64
WINDOW = 128
N_HEADS = 16
N_KV_HEADS = 2
HEAD_DIM = 64
GQA_GROUP = N_HEADS // N_KV_HEADS
D_ATTN = N_HEADS * HEAD_DIM
D_KV = N_KV_HEADS * HEAD_DIM
N_BUCKETS = 32
MAX_DISTANCE = 128
SGU_BLOCK = 128
SGU_GROUPS = 8
SGU_GROUP_DIM = 128
D_SGU = SGU_GROUPS * SGU_GROUP_DIM
PEER_HEADS = 8
PEER_N_KEYS = 128
PEER_HALF = 128
PEER_TOPK = 16
EPS = 1e-6
NEG_INF = -1e30

LANES = 128
SUBLANES = 8
VMEM_LIMIT_BYTES = 56 * 1024 * 1024

_CANDS = tuple((k, l) for k in range(PEER_TOPK) for l in range(PEER_TOPK)
               if (k + 1) * (l + 1) <= PEER_TOPK)


def _cparams(*sem):
    return pltpu.CompilerParams(dimension_semantics=sem, vmem_limit_bytes=VMEM_LIMIT_BYTES)


def _gelu_tanh(x):
    c = math.sqrt(2.0 / math.pi)
    t = jnp.tanh(x * (c + (c * 0.044715) * (x * x)))
    hx = 0.5 * x
    return hx + hx * t


def _rms(x, g):
    r = lax.rsqrt(jnp.mean(x * x, axis=-1, keepdims=True) + EPS)
    return x * r * g


def _norm_kernel(x_ref, g_ref, o_ref):
    o_ref[...] = _rms(x_ref[...], g_ref[...]).astype(o_ref.dtype)


def _norm_cast(x, g, tm):
    t, d = x.shape
    return pl.pallas_call(
        _norm_kernel,
        grid=(t // tm,),
        in_specs=[pl.BlockSpec((tm, d), lambda i: (i, 0)),
                  pl.BlockSpec((1, d), lambda i: (0, 0))],
        out_specs=pl.BlockSpec((tm, d), lambda i: (i, 0)),
        out_shape=jax.ShapeDtypeStruct((t, d), BF16),
        compiler_params=_cparams("parallel"),
        name="rmsnorm_cast",
    )(x, g.reshape(1, d))


def _mm_kernel(x_ref, w_ref, *rest, epilogue):
    acc = jnp.dot(x_ref[...], w_ref[...], preferred_element_type=F32)
    if epilogue == "scale_q":
        (o_ref,) = rest
        o_ref[...] = (acc * (HEAD_DIM ** -0.5)).astype(o_ref.dtype)
    elif epilogue == "none":
        (o_ref,) = rest
        o_ref[...] = acc.astype(o_ref.dtype)
    elif epilogue == "gelu":
        (o_ref,) = rest
        o_ref[...] = jax.nn.gelu(acc).astype(o_ref.dtype)
    elif epilogue == "gelu_norm":
        g_ref, o_ref = rest
        o_ref[...] = _rms(jax.nn.gelu(acc), g_ref[...]).astype(o_ref.dtype)
    elif epilogue == "sigmoid":
        (o_ref,) = rest
        o_ref[...] = jax.nn.sigmoid(acc).astype(o_ref.dtype)
    else:
        raise ValueError(epilogue)


def _mm(x, w, epilogue, out_dtype, tm, tn, g=None, name="mm"):
    t, k = x.shape
    n = w.shape[1]
    in_specs = [pl.BlockSpec((tm, k), lambda i, j: (i, 0)),
                pl.BlockSpec((k, tn), lambda i, j: (0, j))]
    args = [x, w]
    if g is not None:
        in_specs.append(pl.BlockSpec((1, tn), lambda i, j: (0, j)))
        args.append(g.reshape(1, n))
    return pl.pallas_call(
        functools.partial(_mm_kernel, epilogue=epilogue),
        grid=(t // tm, n // tn),
        in_specs=in_specs,
        out_specs=pl.BlockSpec((tm, tn), lambda i, j: (i, j)),
        out_shape=jax.ShapeDtypeStruct((t, n), out_dtype),
        compiler_params=_cparams("parallel", "parallel"),
        name=name,
    )(*args)


def _bias_kernel(table_ref, bucket_ref, o_ref):
    bucket = bucket_ref[...]
    for h in range(N_HEADS):
        acc = jnp.zeros(bucket.shape, F32)
        for b in range(N_BUCKETS):
            acc = jnp.where(bucket == b, table_ref[b, h], acc)
        o_ref[h] = acc


def _t5_bucket(rel):
    half = N_BUCKETS // 2
    max_exact = half // 2
    ret = jnp.where(rel > 0, half, 0)
    n = jnp.abs(rel)
    nf = jnp.maximum(n, 1).astype(jnp.float32)
    large = max_exact + (jnp.log(nf / max_exact) / math.log(MAX_DISTANCE / max_exact)
                         * (half - max_exact)).astype(jnp.int32)
    large = jnp.minimum(large, half - 1)
    return ret + jnp.where(n < max_exact, n, large)


def _rel_bias(table, n_q, n_k):
    rel = (jnp.arange(n_k, dtype=jnp.int32)[None, :] - WINDOW
           - jnp.arange(n_q, dtype=jnp.int32)[:, None])
    bucket = _t5_bucket(rel).astype(jnp.int32)
    return pl.pallas_call(
        _bias_kernel,
        in_specs=[pl.BlockSpec(memory_space=pltpu.SMEM),
                  pl.BlockSpec(memory_space=pltpu.VMEM)],
        out_specs=pl.BlockSpec(memory_space=pltpu.VMEM),
        out_shape=jax.ShapeDtypeStruct((N_HEADS, n_q, n_k), F32),
        name="rel_bias",
    )(table.astype(F32), bucket)


def _attn_kernel(sinks_ref, q_ref, k_ref, v_ref, bias_ref, o_ref, *, n_q, n_k, n_pad):
    c = pl.program_id(1)
    start = pl.multiple_of(c * n_q, n_q)
    kb = k_ref[0, pl.ds(start, n_k), :]
    vb = v_ref[0, pl.ds(start, n_k), :]
    key_row = start + lax.broadcasted_iota(jnp.int32, (1, n_k), 1)
    valid = key_row >= n_pad
    heads = range(N_HEADS)
    kv_cols = [slice((h // GQA_GROUP) * HEAD_DIM, (h // GQA_GROUP + 1) * HEAD_DIM) for h in heads]
    s = [lax.dot_general(q_ref[0, :, h * HEAD_DIM:(h + 1) * HEAD_DIM], kb[:, kv_cols[h]],
                         (((1,), (1,)), ((), ())), preferred_element_type=F32) for h in heads]
    s = [jnp.where(valid, s[h] + bias_ref[h], NEG_INF) for h in heads]
    m = [jnp.maximum(jnp.max(s[h], axis=-1, keepdims=True), sinks_ref[h]) for h in heads]
    p = [jnp.exp(s[h] - m[h]) for h in heads]
    denom = [jnp.sum(p[h], axis=-1, keepdims=True) + jnp.exp(sinks_ref[h] - m[h]) for h in heads]
    o = [jnp.dot(p[h].astype(BF16), vb[:, kv_cols[h]], preferred_element_type=F32) for h in heads]
    for h in heads:
        o_ref[0, :, h * HEAD_DIM:(h + 1) * HEAD_DIM] = (o[h] / denom[h]).astype(o_ref.dtype)


def _attention(q, k, v, bias, sinks, n_q, n_k, n_pad):
    nb, sq, _ = q.shape
    rows = k.shape[1]
    return pl.pallas_call(
        functools.partial(_attn_kernel, n_q=n_q, n_k=n_k, n_pad=n_pad),
        grid=(nb, sq // n_q),
        in_specs=[pl.BlockSpec(memory_space=pltpu.SMEM),
                  pl.BlockSpec((1, n_q, D_ATTN), lambda b, c: (b, c, 0)),
                  pl.BlockSpec((1, rows, D_KV), lambda b, c: (b, 0, 0)),
                  pl.BlockSpec((1, rows, D_KV), lambda b, c: (b, 0, 0)),
                  pl.BlockSpec((N_HEADS, n_q, n_k), lambda b, c: (0, 0, 0))],
        out_specs=pl.BlockSpec((1, n_q, D_ATTN), lambda b, c: (b, c, 0)),
        out_shape=jax.ShapeDtypeStruct((nb, sq, D_ATTN), BF16),
        compiler_params=_cparams("parallel", "arbitrary"),
        name="swa_attention",
    )(sinks.astype(F32), q, k, v, bias)


def _sgu_kernel(u_ref, v_ref, w_ref, b_ref, o_ref, *, rows):
    r = lax.broadcasted_iota(jnp.int32, (rows, rows), 0)
    c = lax.broadcasted_iota(jnp.int32, (rows, rows), 1)
    tril = c <= r
    for g in range(SGU_GROUPS):
        sl = slice(g * SGU_GROUP_DIM, (g + 1) * SGU_GROUP_DIM)
        w = jnp.where(tril, w_ref[g], 0.0).astype(BF16)
        mixed = jnp.dot(w, v_ref[0, :, sl].astype(BF16), preferred_element_type=F32)
        o_ref[0, :, sl] = (u_ref[0, :, sl].astype(F32) * (mixed + b_ref[:, sl])).astype(o_ref.dtype)


def _sgu(u, vn, w_s, b_s, rows):
    nblk = u.shape[0]
    w = w_s[:, :rows, :rows]
    b_full = jnp.repeat(b_s[:, :rows].T, SGU_GROUP_DIM, axis=1).astype(F32)
    return pl.pallas_call(
        functools.partial(_sgu_kernel, rows=rows),
        grid=(nblk,),
        in_specs=[pl.BlockSpec((1, rows, D_SGU), lambda i: (i, 0, 0)),
                  pl.BlockSpec((1, rows, D_SGU), lambda i: (i, 0, 0)),
                  pl.BlockSpec((SGU_GROUPS, rows, rows), lambda i: (0, 0, 0)),
                  pl.BlockSpec((rows, D_SGU), lambda i: (0, 0))],
        out_specs=pl.BlockSpec((1, rows, D_SGU), lambda i: (i, 0, 0)),
        out_shape=jax.ShapeDtypeStruct((nblk, rows, D_SGU), BF16),
        compiler_params=_cparams("parallel"),
        name="sgu",
    )(u, vn, w, b_full)


def _merge_kernel(o_ref, s_ref, gate_ref, x_ref, wpa_ref, wpb_ref, wout_ref, gn_ref,
                  xo_ref, xn_ref):
    d = x_ref.shape[-1]
    ta = jnp.dot(o_ref[...], wpa_ref[...], preferred_element_type=F32)
    tb = jnp.dot(s_ref[...], wpb_ref[...], preferred_element_type=F32)
    h = gate_ref[:, :d].astype(F32) * ta + gate_ref[:, d:].astype(F32) * tb
    y = x_ref[...] + jnp.dot(h.astype(BF16), wout_ref[...], preferred_element_type=F32)
    xo_ref[...] = y
    xn_ref[...] = _rms(y, gn_ref[...]).astype(xn_ref.dtype)


def _merge(o, s, gates, x, w_pa, w_pb, w_out, g_ffn, tm):
    t, d = x.shape
    once = pl.Buffered(1)
    return pl.pallas_call(
        _merge_kernel,
        grid=(t // tm,),
        in_specs=[pl.BlockSpec((tm, D_ATTN), lambda i: (i, 0)),
                  pl.BlockSpec((tm, D_SGU), lambda i: (i, 0)),
                  pl.BlockSpec((tm, 2 * d), lambda i: (i, 0)),
                  pl.BlockSpec((tm, d), lambda i: (i, 0)),
                  pl.BlockSpec((D_ATTN, d), lambda i: (0, 0), pipeline_mode=once),
                  pl.BlockSpec((D_SGU, d), lambda i: (0, 0), pipeline_mode=once),
                  pl.BlockSpec((d, d), lambda i: (0, 0), pipeline_mode=once),
                  pl.BlockSpec((1, d), lambda i: (0, 0))],
        out_specs=[pl.BlockSpec((tm, d), lambda i: (i, 0)),
                   pl.BlockSpec((tm, d), lambda i: (i, 0))],
        out_shape=[jax.ShapeDtypeStruct((t, d), F32),
                   jax.ShapeDtypeStruct((t, d), BF16)],
        compiler_params=_cparams("parallel"),
        name="merge",
    )(o, s, gates, x, w_pa, w_pb, w_out, g_ffn.reshape(1, d))


def _bf16_pair(x):
    bits = pltpu.bitcast(x.astype(BF16).astype(F32), jnp.uint32)
    return bits | (bits >> 16)


def _peer_topk_kernel(q_ref, keys_ref, rankb_out_ref, e1_ref, nsel_ref, coef_ref,
                      s_ref, ranka_ref, rankb_ref, top0_ref, top1_ref, cand_ref, cnt_ref, zinv_ref):
    nk = PEER_N_KEYS
    tb = q_ref.shape[0]
    q = q_ref[...]
    for hp in range(2 * PEER_HEADS):
        s_ref[hp * nk:(hp + 1) * nk, :] = lax.dot_general(
            keys_ref[hp], q[:, hp * PEER_HALF:(hp + 1) * PEER_HALF],
            (((1,), (1,)), ((), ())), preferred_element_type=F32)

    key_id = lax.broadcasted_iota(jnp.int32, (nk, tb), 0).astype(F32)
    for h in range(PEER_HEADS):
        rows = slice(h * nk, (h + 1) * nk)
        ranka_ref[rows, :] = jnp.full((nk, tb), float(PEER_TOPK), F32)
        rankb_ref[rows, :] = jnp.full((nk, tb), float(PEER_TOPK), F32)

        def body(r, works, h=h, rows=rows):
            rf = jnp.asarray(r, F32)
            out = []
            for w, rank_ref, top_ref in zip(works, (ranka_ref, rankb_ref), (top0_ref, top1_ref)):
                m = jnp.max(w, axis=0, keepdims=True)
                first = jnp.min(jnp.where(w == m, key_id, float(nk)), axis=0, keepdims=True)
                hit = key_id == first
                out.append(jnp.where(hit, -jnp.inf, w))
                rank_ref[rows, :] = jnp.where(hit, rf, rank_ref[rows, :])
                top_ref[r, pl.ds(h, 1), :] = m
            return tuple(out)

        lax.fori_loop(0, PEER_TOPK, body,
                      (s_ref[2 * h * nk:(2 * h + 1) * nk, :],
                       s_ref[(2 * h + 1) * nk:(2 * h + 2) * nk, :]))

    for c, (k, l) in enumerate(_CANDS):
        cand_ref[c] = top0_ref[k] + top1_ref[l]
    for k in range(PEER_TOPK):
        cnt_ref[k] = jnp.zeros(cnt_ref.shape[1:], F32)
    cv0 = top0_ref[0] + top1_ref[0]
    big = float(PEER_TOPK * PEER_TOPK)

    def body(r, z):
        m = cand_ref[0]
        for c in range(1, len(_CANDS)):
            m = jnp.maximum(m, cand_ref[c])
        idx = jnp.full(m.shape, big, F32)
        for c, (k, l) in enumerate(_CANDS):
            idx = jnp.minimum(idx, jnp.where(cand_ref[c] == m, float(k * PEER_TOPK + l), big))
        for k in range(PEER_TOPK):
            cnt = cnt_ref[k]
            for c, (kc, l) in enumerate(_CANDS):
                if kc != k:
                    continue
                hit = idx == float(k * PEER_TOPK + l)
                cand_ref[c] = jnp.where(hit, -jnp.inf, cand_ref[c])
                cnt = cnt + jnp.where(hit, 1.0, 0.0)
            cnt_ref[k] = cnt
        return z + jnp.exp(m - cv0)

    z = lax.fori_loop(0, PEER_TOPK, body, jnp.zeros(cv0.shape, F32))
    zinv_ref[...] = 1.0 / z
    for h in range(PEER_HEADS):
        rows = slice(h * nk, (h + 1) * nk)
        head = pl.ds(h, 1)
        ra = ranka_ref[rows, :]
        nsel = jnp.zeros((nk, tb), F32)
        for k in range(PEER_TOPK):
            nsel = jnp.where(ra == float(k), cnt_ref[k, head, :], nsel)
        nsel_ref[h] = _bf16_pair(nsel)
        s0 = s_ref[2 * h * nk:(2 * h + 1) * nk, :]
        s1 = s_ref[(2 * h + 1) * nk:(2 * h + 2) * nk, :]
        coef_ref[h] = _bf16_pair(jnp.exp(s0 - top0_ref[0, head, :]) * zinv_ref[head, :])
        e1_ref[rows, :] = jnp.exp(s1 - top1_ref[0, head, :]).astype(e1_ref.dtype)
        rankb_out_ref[rows, :] = rankb_ref[rows, :].astype(rankb_out_ref.dtype)


def _peer_topk(q, keys, tb):
    t = q.shape[0]
    rows = PEER_HEADS * PEER_N_KEYS
    out_w = jax.ShapeDtypeStruct((PEER_HEADS, PEER_N_KEYS, t), jnp.uint32)
    out_b = jax.ShapeDtypeStruct((rows, t), BF16)
    ospec = pl.BlockSpec((rows, tb), lambda i: (0, i))
    wspec = pl.BlockSpec((PEER_HEADS, PEER_N_KEYS, tb), lambda i: (0, 0, i))
    hv = (PEER_HEADS, tb)
    return pl.pallas_call(
        _peer_topk_kernel,
        grid=(t // tb,),
        in_specs=[pl.BlockSpec((tb, q.shape[1]), lambda i: (i, 0)),
                  pl.BlockSpec(keys.shape, lambda i: (0, 0, 0))],
        out_specs=[ospec, ospec, wspec, wspec],
        out_shape=[out_b, out_b, out_w, out_w],
        scratch_shapes=[pltpu.VMEM((2 * rows, tb), F32),
                        pltpu.VMEM((rows, tb), F32),
                        pltpu.VMEM((rows, tb), F32),
                        pltpu.VMEM((PEER_TOPK,) + hv, F32),
                        pltpu.VMEM((PEER_TOPK,) + hv, F32),
                        pltpu.VMEM((len(_CANDS),) + hv, F32),
                        pltpu.VMEM((PEER_TOPK,) + hv, F32),
                        pltpu.VMEM(hv, F32)],
        compiler_params=_cparams("parallel"),
        name="peer_topk",
    )(q, keys)


def _peer_expert_kernel(xt_ref, u_ref, vt_ref, rankb_ref, e1_ref, nsel_ref, coef_ref,
                        xres_ref, gf_ref, y_ref, acc_ref, *, n_chunks, i_per_chunk, ahead,
                        final_norm):
    et = pl.program_id(1)
    nk = PEER_N_KEYS
    ck = i_per_chunk * nk

    @pl.when(et == 0)
    def _():
        acc_ref[...] = jnp.zeros_like(acc_ref)

    def rows_of(word_ref, h, il):
        words = jnp.broadcast_to(word_ref[h, pl.ds(il, 1), :], (SUBLANES, word_ref.shape[-1]))
        packed = pltpu.bitcast(words, BF16)
        return jnp.tile(packed, (nk // packed.shape[0], 1))

    def up_proj(c):
        return jnp.dot(u_ref[c * ck:(c + 1) * ck, :], xt_ref[...],
                       preferred_element_type=F32)

    a_next = [up_proj(c) for c in range(min(ahead, n_chunks))]
    zero = jnp.zeros((), BF16)
    for c in range(n_chunks):
        rows = slice(c * ck, (c + 1) * ck)
        if c + ahead < n_chunks:
            a_next.append(up_proj(c + ahead))
        a_t = a_next[c]
        hidden = []
        for ic in range(i_per_chunk):
            il = c * i_per_chunk + ic
            gate = None
            for h in range(PEER_HEADS):
                hs = slice(h * nk, (h + 1) * nk)
                sel = jnp.where(rankb_ref[hs, :] < rows_of(nsel_ref, h, il), e1_ref[hs, :], zero)
                sel = sel * rows_of(coef_ref, h, il)
                gate = sel if gate is None else gate + sel
            act = _gelu_tanh(a_t[ic * nk:(ic + 1) * nk, :]).astype(BF16)
            hidden.append(gate * act)
        acc_ref[...] += jnp.dot(vt_ref[:, rows], jnp.concatenate(hidden, axis=0),
                                preferred_element_type=F32)

    @pl.when(et == pl.num_programs(1) - 1)
    def _():
        y = xres_ref[...] + acc_ref[...].T
        if final_norm:
            y = _rms(y, gf_ref[...])
        y_ref[...] = y


def _peer_experts(xn_t, u, v_t, rankb, e1, nsel, coef, xres, g_final, tb, e_tile, i_per_chunk,
                  ahead, final_norm):
    d, t = xn_t.shape
    rows = PEER_HEADS * PEER_N_KEYS
    aux = pl.BlockSpec((rows, tb), lambda i, j: (0, i))
    per_tile = pl.BlockSpec((PEER_HEADS, e_tile // PEER_N_KEYS, tb), lambda i, j: (0, j, i))
    return pl.pallas_call(
        functools.partial(_peer_expert_kernel, n_chunks=e_tile // (i_per_chunk * PEER_N_KEYS),
                          i_per_chunk=i_per_chunk, ahead=ahead, final_norm=final_norm),
        grid=(t // tb, u.shape[0] // e_tile),
        in_specs=[pl.BlockSpec((d, tb), lambda i, j: (0, i)),
                  pl.BlockSpec((e_tile, d), lambda i, j: (j, 0)),
                  pl.BlockSpec((d, e_tile), lambda i, j: (0, j)),
                  aux, aux, per_tile, per_tile,
                  pl.BlockSpec((tb, d), lambda i, j: (i, 0)),
                  pl.BlockSpec((1, d), lambda i, j: (0, 0))],
        out_specs=pl.BlockSpec((tb, d), lambda i, j: (i, 0)),
        out_shape=jax.ShapeDtypeStruct((t, d), F32),
        scratch_shapes=[pltpu.VMEM((d, tb), F32)],
        compiler_params=_cparams("parallel", "arbitrary"),
        name="peer_experts",
    )(xn_t, u, v_t, rankb, e1, nsel, coef, xres, g_final.reshape(1, d))


def _row_tile(t, pref):
    return pref if t % pref == 0 else t


def _in_projection(x2, g_mix, w_in, sgu_g):
    t = x2.shape[0]
    xn = _norm_cast(x2, g_mix, _row_tile(t, 512))
    tm = _row_tile(t, 1024)
    c0, c1, c2, c3, c4 = D_ATTN, D_ATTN + 2 * D_KV, D_ATTN + 2 * D_KV + D_SGU, \
        D_ATTN + 2 * D_KV + 2 * D_SGU, w_in.shape[1]
    q = _mm(xn, w_in[:, :c0], "scale_q", BF16, tm, 1024, name="proj_q")
    kv = _mm(xn, w_in[:, c0:c1], "none", F32, tm, 2 * D_KV, name="proj_kv")
    u = _mm(xn, w_in[:, c1:c2], "gelu", BF16, tm, 1024, name="proj_u")
    vs = _mm(xn, w_in[:, c2:c3], "gelu_norm", F32, _row_tile(t, 512), D_SGU, g=sgu_g, name="proj_vsgu")
    gates = _mm(xn, w_in[:, c3:c4], "sigmoid", BF16, tm, 1024, name="proj_gates")
    return q, kv, u, vs, gates


def _peer(xn, xres, w_query, keys, u, v_t, g_final, final_norm):
    t = xn.shape[0]
    q = _mm(xn, w_query, "none", BF16, _row_tile(t, 1024), 1024, name="peer_query")
    rankb, e1, nsel, coef = _peer_topk(q, keys, LANES)
    tb = _row_tile(t, 512)
    return _peer_experts(xn.T, u, v_t, rankb, e1, nsel, coef, xres, g_final, tb, 1024, 2, 2,
                         final_norm)


def kernel(x_prompt, x_sample, cache_k_swa, cache_v_swa, norm_mix_g, w_in, sgu_norm_g, sgu_w_s,
           sgu_b_s, attn_sinks, rel_bias_table, w_branch_attn, w_branch_sgu, w_out, norm_ffn_g,
           peer_w_query, peer_sub_keys, peer_expert_u, peer_expert_v, norm_final_g):
    B, S, D = x_prompt.shape
    Bd, T, _ = x_sample.shape
    depth = w_in.shape[0]
    bias_p = _rel_bias(rel_bias_table, CHUNK, WINDOW + CHUNK)
    bias_s = _rel_bias(rel_bias_table, T, WINDOW + T)

    xp = x_prompt.reshape(B * S, D)
    xs = x_sample.reshape(Bd * T, D)
    nk_p, nv_p, nk_s, nv_s, nsgu_s = [], [], [], [], []
    for l in range(depth):
        last = l == depth - 1
        w_in_l = w_in[l].astype(BF16)
        w_pa = w_branch_attn[l].astype(BF16)
        w_pb = w_branch_sgu[l].astype(BF16)
        w_o = w_out[l].astype(BF16)
        w_q = peer_w_query[l].astype(BF16)
        keys = peer_sub_keys[l].astype(BF16).reshape(2 * PEER_HEADS, PEER_N_KEYS, PEER_HALF)
        e_u = peer_expert_u[l].astype(BF16)
        e_vt = peer_expert_v[l].astype(BF16).T
        g_out = norm_final_g if last else jnp.ones_like(norm_final_g)

        q, kv, u, vs, gates = _in_projection(xp, norm_mix_g[l], w_in_l, sgu_norm_g[l])
        kv3 = kv.reshape(B, S, 2 * D_KV)
        kpad = jnp.pad(kv3[..., :D_KV].astype(BF16), ((0, 0), (WINDOW, 0), (0, 0)))
        vpad = jnp.pad(kv3[..., D_KV:].astype(BF16), ((0, 0), (WINDOW, 0), (0, 0)))
        o = _attention(q.reshape(B, S, D_ATTN), kpad, vpad, bias_p, attn_sinks[l],
                       CHUNK, WINDOW + CHUNK, WINDOW)
        n_blk = S // SGU_BLOCK
        sg = _sgu(u.reshape(B * n_blk, SGU_BLOCK, D_SGU), vs.reshape(B * n_blk, SGU_BLOCK, D_SGU),
                  sgu_w_s[l], sgu_b_s[l], SGU_BLOCK)
        xp, xnp = _merge(o.reshape(B * S, D_ATTN), sg.reshape(B * S, D_SGU), gates, xp,
                         w_pa, w_pb, w_o, norm_ffn_g[l], _row_tile(B * S, 256))
        nk_p.append(kv3[:, S - WINDOW:, :D_KV].reshape(B, WINDOW, N_KV_HEADS, HEAD_DIM))
        nv_p.append(kv3[:, S - WINDOW:, D_KV:].reshape(B, WINDOW, N_KV_HEADS, HEAD_DIM))

        qs, kvs, us, vss, gates_s = _in_projection(xs, norm_mix_g[l], w_in_l, sgu_norm_g[l])
        kvs3 = kvs.reshape(Bd, T, 2 * D_KV)
        k_all = jnp.concatenate([cache_k_swa[l].reshape(Bd, WINDOW, D_KV).astype(BF16),
                                 kvs3[..., :D_KV].astype(BF16)], axis=1)
        v_all = jnp.concatenate([cache_v_swa[l].reshape(Bd, WINDOW, D_KV).astype(BF16),
                                 kvs3[..., D_KV:].astype(BF16)], axis=1)
        os_ = _attention(qs.reshape(Bd, T, D_ATTN), k_all, v_all, bias_s, attn_sinks[l],
                         T, WINDOW + T, 0)
        ss = _sgu(us.reshape(Bd, T, D_SGU), vss.reshape(Bd, T, D_SGU), sgu_w_s[l], sgu_b_s[l], T)
        xs, xns = _merge(os_.reshape(Bd * T, D_ATTN), ss.reshape(Bd * T, D_SGU), gates_s, xs,
                         w_pa, w_pb, w_o, norm_ffn_g[l], Bd * T)
        nk_s.append(kvs3[..., :D_KV].reshape(Bd, T, N_KV_HEADS, HEAD_DIM))
        nv_s.append(kvs3[..., D_KV:].reshape(Bd, T, N_KV_HEADS, HEAD_DIM))
        nsgu_s.append(vss.reshape(Bd, T, D_SGU))

        xp = _peer(xnp, xp, w_q, keys, e_u, e_vt, g_out, last)
        xs = _peer(xns, xs, w_q, keys, e_u, e_vt, g_out, last)

    return (xp.reshape(B, S, D), xs.reshape(Bd, T, D), jnp.stack(nk_p), jnp.stack(nv_p),
            jnp.stack(nk_s), jnp.stack(nv_s), jnp.stack(nsgu_s))
```

```python
import functools
import math

import jax
import jax.numpy as jnp
import numpy as np
from jax import lax
from jax.experimental import pallas as pl
from jax.experimental.pallas import tpu as pltpu

F32 = jnp.float32
BF16 = jnp.bfloat16

CHUNK = 64
WINDOW = 128
N_HEADS = 16
N_KV_HEADS = 2
HEAD_DIM = 64
GQA_GROUP = N_HEADS // N_KV_HEADS
D_ATTN = N_HEADS * HEAD_DIM
D_KV = N_KV_HEADS * HEAD_DIM
N_BUCKETS = 32
MAX_DISTANCE = 128
SGU_BLOCK = 128
SGU_GROUPS = 8
SGU_GROUP_DIM = 128
D_SGU = SGU_GROUPS * SGU_GROUP_DIM
PEER_HEADS = 8
PEER_N_KEYS = 128
PEER_HALF = 128
PEER_TOPK = 16
EPS = 1e-6
NEG_INF = -1e30

LANES = 128
SUBLANES = 8
VMEM_LIMIT_BYTES = 56 * 1024 * 1024

_CANDS = tuple((k, l) for k in range(PEER_TOPK) for l in range(PEER_TOPK)
               if (k + 1) * (l + 1) <= PEER_TOPK)


def _cparams(*sem):
    return pltpu.CompilerParams(dimension_semantics=sem, vmem_limit_bytes=VMEM_LIMIT_BYTES)


def _gelu_tanh(x):
    c = math.sqrt(2.0 / math.pi)
    t = jnp.tanh(x * (c + (c * 0.044715) * (x * x)))
    hx = 0.5 * x
    return hx + hx * t


def _rms(x, g):
    r = lax.rsqrt(jnp.mean(x * x, axis=-1, keepdims=True) + EPS)
    return x * r * g


def _norm_kernel(x_ref, g_ref, o_ref):
    o_ref[...] = _rms(x_ref[...], g_ref[...]).astype(o_ref.dtype)


def _norm_cast(x, g, tm):
    t, d = x.shape
    return pl.pallas_call(
        _norm_kernel,
        grid=(t // tm,),
        in_specs=[pl.BlockSpec((tm, d), lambda i: (i, 0)),
                  pl.BlockSpec((1, d), lambda i: (0, 0))],
        out_specs=pl.BlockSpec((tm, d), lambda i: (i, 0)),
        out_shape=jax.ShapeDtypeStruct((t, d), BF16),
        compiler_params=_cparams("parallel"),
        name="rmsnorm_cast",
    )(x, g.reshape(1, d))


def _mm_kernel(x_ref, w_ref, *rest, epilogue):
    acc = jnp.dot(x_ref[...], w_ref[...], preferred_element_type=F32)
    if epilogue == "scale_q":
        (o_ref,) = rest
        o_ref[...] = (acc * (HEAD_DIM ** -0.5)).astype(o_ref.dtype)
    elif epilogue == "none":
        (o_ref,) = rest
        o_ref[...] = acc.astype(o_ref.dtype)
    elif epilogue == "gelu":
        (o_ref,) = rest
        o_ref[...] = jax.nn.gelu(acc).astype(o_ref.dtype)
    elif epilogue == "gelu_norm":
        g_ref, o_ref = rest
        o_ref[...] = _rms(jax.nn.gelu(acc), g_ref[...]).astype(o_ref.dtype)
    elif epilogue == "sigmoid":
        (o_ref,) = rest
        o_ref[...] = jax.nn.sigmoid(acc).astype(o_ref.dtype)
    else:
        raise ValueError(epilogue)


def _mm(x, w, epilogue, out_dtype, tm, tn, g=None, name="mm"):
    t, k = x.shape
    n = w.shape[1]
    in_specs = [pl.BlockSpec((tm, k), lambda i, j: (i, 0)),
                pl.BlockSpec((k, tn), lambda i, j: (0, j))]
    args = [x, w]
    if g is not None:
        in_specs.append(pl.BlockSpec((1, tn), lambda i, j: (0, j)))
        args.append(g.reshape(1, n))
    return pl.pallas_call(
        functools.partial(_mm_kernel, epilogue=epilogue),
        grid=(t // tm, n // tn),
        in_specs=in_specs,
        out_specs=pl.BlockSpec((tm, tn), lambda i, j: (i, j)),
        out_shape=jax.ShapeDtypeStruct((t, n), out_dtype),
        compiler_params=_cparams("parallel", "parallel"),
        name=name,
    )(*args)


def _bias_kernel(table_ref, bucket_ref, o_ref):
    bucket = bucket_ref[...]
    for h in range(N_HEADS):
        acc = jnp.zeros(bucket.shape, F32)
        for b in range(N_BUCKETS):
            acc = jnp.where(bucket == b, table_ref[b, h], acc)
        o_ref[h] = acc


def _t5_bucket(rel):
    half = N_BUCKETS // 2
    max_exact = half // 2
    ret = jnp.where(rel > 0, half, 0)
    n = jnp.abs(rel)
    nf = jnp.maximum(n, 1).astype(jnp.float32)
    large = max_exact + (jnp.log(nf / max_exact) / math.log(MAX_DISTANCE / max_exact)
                         * (half - max_exact)).astype(jnp.int32)
    large = jnp.minimum(large, half - 1)
    return ret + jnp.where(n < max_exact, n, large)


def _rel_bias(table, n_q, n_k):
    rel = (jnp.arange(n_k, dtype=jnp.int32)[None, :] - WINDOW
           - jnp.arange(n_q, dtype=jnp.int32)[:, None])
    bucket = _t5_bucket(rel).astype(jnp.int32) & (N_BUCKETS - 1)
    return pl.pallas_call(
        _bias_kernel,
        in_specs=[pl.BlockSpec(memory_space=pltpu.SMEM),
                  pl.BlockSpec(memory_space=pltpu.VMEM)],
        out_specs=pl.BlockSpec(memory_space=pltpu.VMEM),
        out_shape=jax.ShapeDtypeStruct((N_HEADS, n_q, n_k), F32),
        name="rel_bias",
    )(table.astype(F32), bucket)


def _attn_kernel(sinks_ref, q_ref, k_ref, v_ref, bias_ref, o_ref, *, n_q, n_k, n_pad):
    c = pl.program_id(1)
    start = pl.multiple_of(c * n_q, n_q)
    kb = k_ref[0, pl.ds(start, n_k), :]
    vb = v_ref[0, pl.ds(start, n_k), :]
    key_row = start + lax.broadcasted_iota(jnp.int32, (1, n_k), 1)
    valid = key_row >= n_pad
    heads = range(N_HEADS)
    kv_cols = [slice((h // GQA_GROUP) * HEAD_DIM, (h // GQA_GROUP + 1) * HEAD_DIM) for h in heads]
    s = [lax.dot_general(q_ref[0, :, h * HEAD_DIM:(h + 1) * HEAD_DIM], kb[:, kv_cols[h]],
                         (((1,), (1,)), ((), ())), preferred_element_type=F32) for h in heads]
    s = [jnp.where(valid, s[h] + bias_ref[h], NEG_INF) for h in heads]
    m = [jnp.maximum(jnp.max(s[h], axis=-1, keepdims=True), sinks_ref[h]) for h in heads]
    p = [jnp.exp(s[h] - m[h]) for h in heads]
    denom = [jnp.sum(p[h], axis=-1, keepdims=True) + jnp.exp(sinks_ref[h] - m[h]) for h in heads]
    o = [jnp.dot(p[h].astype(BF16), vb[:, kv_cols[h]], preferred_element_type=F32) for h in heads]
    for h in heads:
        o_ref[0, :, h * HEAD_DIM:(h + 1) * HEAD_DIM] = (o[h] / denom[h]).astype(o_ref.dtype)


def _attention(q, k, v, bias, sinks, n_q, n_k, n_pad):
    nb, sq, _ = q.shape
    rows = k.shape[1]
    return pl.pallas_call(
        functools.partial(_attn_kernel, n_q=n_q, n_k=n_k, n_pad=n_pad),
        grid=(nb, sq // n_q),
        in_specs=[pl.BlockSpec(memory_space=pltpu.SMEM),
                  pl.BlockSpec((1, n_q, D_ATTN), lambda b, c: (b, c, 0)),
                  pl.BlockSpec((1, rows, D_KV), lambda b, c: (b, 0, 0)),
                  pl.BlockSpec((1, rows, D_KV), lambda b, c: (b, 0, 0)),
                  pl.BlockSpec((N_HEADS, n_q, n_k), lambda b, c: (0, 0, 0))],
        out_specs=pl.BlockSpec((1, n_q, D_ATTN), lambda b, c: (b, c, 0)),
        out_shape=jax.ShapeDtypeStruct((nb, sq, D_ATTN), BF16),
        compiler_params=_cparams("parallel", "arbitrary"),
        name="swa_attention",
    )(sinks.astype(F32), q, k, v, bias)


def _sgu_kernel(u_ref, v_ref, w_ref, b_ref, o_ref, *, rows):
    r = lax.broadcasted_iota(jnp.int32, (rows, rows), 0)
    c = lax.broadcasted_iota(jnp.int32, (rows, rows), 1)
    tril = c <= r
    for g in range(SGU_GROUPS):
        sl = slice(g * SGU_GROUP_DIM, (g + 1) * SGU_GROUP_DIM)
        w = jnp.where(tril, w_ref[g], 0.0).astype(BF16)
        mixed = jnp.dot(w, v_ref[0, :, sl].astype(BF16), preferred_element_type=F32)
        o_ref[0, :, sl] = (u_ref[0, :, sl].astype(F32) * (mixed + b_ref[:, sl])).astype(o_ref.dtype)


def _sgu(u, vn, w_s, b_s, rows):
    nblk = u.shape[0]
    w = w_s[:, :rows, :rows]
    b_full = jnp.repeat(b_s[:, :rows].T, SGU_GROUP_DIM, axis=1).astype(F32)
    return pl.pallas_call(
        functools.partial(_sgu_kernel, rows=rows),
        grid=(nblk,),
        in_specs=[pl.BlockSpec((1, rows, D_SGU), lambda i: (i, 0, 0)),
                  pl.BlockSpec((1, rows, D_SGU), lambda i: (i, 0, 0)),
                  pl.BlockSpec((SGU_GROUPS, rows, rows), lambda i: (0, 0, 0)),
                  pl.BlockSpec((rows, D_SGU), lambda i: (0, 0))],
        out_specs=pl.BlockSpec((1, rows, D_SGU), lambda i: (i, 0, 0)),
        out_shape=jax.ShapeDtypeStruct((nblk, rows, D_SGU), BF16),
        compiler_params=_cparams("parallel"),
        name="sgu",
    )(u, vn, w, b_full)


def _merge_kernel(o_ref, s_ref, gate_ref, x_ref, wpa_ref, wpb_ref, wout_ref, gn_ref,
                  xo_ref, xn_ref):
    d = x_ref.shape[-1]
    ta = jnp.dot(o_ref[...], wpa_ref[...], preferred_element_type=F32)
    tb = jnp.dot(s_ref[...], wpb_ref[...], preferred_element_type=F32)
    h = gate_ref[:, :d].astype(F32) * ta + gate_ref[:, d:].astype(F32) * tb
    y = x_ref[...] + jnp.dot(h.astype(BF16), wout_ref[...], preferred_element_type=F32)
    xo_ref[...] = y
    xn_ref[...] = _rms(y, gn_ref[...]).astype(xn_ref.dtype)


def _merge(o, s, gates, x, w_pa, w_pb, w_out, g_ffn, tm):
    t, d = x.shape
    once = pl.Buffered(1)
    return pl.pallas_call(
        _merge_kernel,
        grid=(t // tm,),
        in_specs=[pl.BlockSpec((tm, D_ATTN), lambda i: (i, 0)),
                  pl.BlockSpec((tm, D_SGU), lambda i: (i, 0)),
                  pl.BlockSpec((tm, 2 * d), lambda i: (i, 0)),
                  pl.BlockSpec((tm, d), lambda i: (i, 0)),
                  pl.BlockSpec((D_ATTN, d), lambda i: (0, 0), pipeline_mode=once),
                  pl.BlockSpec((D_SGU, d), lambda i: (0, 0), pipeline_mode=once),
                  pl.BlockSpec((d, d), lambda i: (0, 0), pipeline_mode=once),
                  pl.BlockSpec((1, d), lambda i: (0, 0))],
        out_specs=[pl.BlockSpec((tm, d), lambda i: (i, 0)),
                   pl.BlockSpec((tm, d), lambda i: (i, 0))],
        out_shape=[jax.ShapeDtypeStruct((t, d), F32),
                   jax.ShapeDtypeStruct((t, d), BF16)],
        compiler_params=_cparams("parallel"),
        name="merge",
    )(o, s, gates, x, w_pa, w_pb, w_out, g_ffn.reshape(1, d))


def _peer_topk_kernel(q_ref, keys_ref, rankb_out_ref, e1_ref, nsel_ref, coef_ref,
                      s_ref, ranka_ref, rankb_ref, top0_ref, top1_ref, cand_ref, cnt_ref, zinv_ref):
    nk = PEER_N_KEYS
    tb = q_ref.shape[0]
    q = q_ref[...]
    for hp in range(2 * PEER_HEADS):
        s_ref[hp * nk:(hp + 1) * nk, :] = lax.dot_general(
            keys_ref[hp], q[:, hp * PEER_HALF:(hp + 1) * PEER_HALF],
            (((1,), (1,)), ((), ())), preferred_element_type=F32)

    key_id = lax.broadcasted_iota(jnp.int32, (nk, tb), 0).astype(F32)
    heads_per_loop = 2
    for h0 in range(0, PEER_HEADS, heads_per_loop):
        heads = range(h0, h0 + heads_per_loop)
        for h in heads:
            rows = slice(h * nk, (h + 1) * nk)
            ranka_ref[rows, :] = jnp.full((nk, tb), float(PEER_TOPK), F32)
            rankb_ref[rows, :] = jnp.full((nk, tb), float(PEER_TOPK), F32)

        def body(r, works, heads=heads):
            rf = jnp.asarray(r, F32)
            out = []
            chains = [(h, rank_ref, top_ref) for h in heads
                      for rank_ref, top_ref in ((ranka_ref, top0_ref), (rankb_ref, top1_ref))]
            for w, (h, rank_ref, top_ref) in zip(works, chains):
                rows = slice(h * nk, (h + 1) * nk)
                m = jnp.max(w, axis=0, keepdims=True)
                first = jnp.min(jnp.where(w == m, key_id, float(nk)), axis=0, keepdims=True)
                hit = key_id == first
                out.append(jnp.where(hit, -jnp.inf, w))
                rank_ref[rows, :] = jnp.where(hit, rf, rank_ref[rows, :])
                top_ref[r, pl.ds(h, 1), :] = m
            return tuple(out)

        lax.fori_loop(0, PEER_TOPK, body,
                      tuple(s_ref[hp * nk:(hp + 1) * nk, :]
                            for hp in range(2 * h0, 2 * (h0 + heads_per_loop))))

    for c, (k, l) in enumerate(_CANDS):
        cand_ref[c] = top0_ref[k] + top1_ref[l]
    cand_at = {kl: c for c, kl in enumerate(_CANDS)}
    row_len = [sum(1 for (k2, _) in _CANDS if k2 == k) for k in range(PEER_TOPK)]
    cv0 = cand_ref[cand_at[(0, 0)]]
    zeros = jnp.zeros(cv0.shape, F32)

    def body(r, carry):
        front, taken, z = carry
        m = functools.reduce(jnp.maximum, front)
        first = functools.reduce(
            jnp.minimum, [jnp.where(front[k] == m, float(k), float(PEER_TOPK))
                          for k in range(PEER_TOPK)])
        new_front, new_taken = [], []
        for k in range(PEER_TOPK):
            hit = first == float(k)
            n_k = taken[k] + jnp.where(hit, 1.0, 0.0)
            nxt = jnp.full(cv0.shape, -jnp.inf, F32)
            for l in range(1, row_len[k]):
                nxt = jnp.where(n_k == float(l), cand_ref[cand_at[(k, l)]], nxt)
            new_front.append(jnp.where(hit, nxt, front[k]))
            new_taken.append(n_k)
        return tuple(new_front), tuple(new_taken), z + jnp.exp(m - cv0)

    front0 = tuple(cand_ref[cand_at[(k, 0)]] for k in range(PEER_TOPK))
    _, taken, z = lax.fori_loop(0, PEER_TOPK, body, (front0, (zeros,) * PEER_TOPK, zeros))
    for k in range(PEER_TOPK):
        cnt_ref[k] = taken[k]
    zinv_ref[...] = 1.0 / z
    for h in range(PEER_HEADS):
        rows = slice(h * nk, (h + 1) * nk)
        head = pl.ds(h, 1)
        ra = ranka_ref[rows, :]
        nsel = jnp.zeros((nk, tb), F32)
        for k in range(PEER_TOPK):
            nsel = jnp.where(ra == float(k), cnt_ref[k, head, :], nsel)
        nsel_ref[h] = nsel
        s0 = s_ref[2 * h * nk:(2 * h + 1) * nk, :]
        s1 = s_ref[(2 * h + 1) * nk:(2 * h + 2) * nk, :]
        coef_ref[h] = jnp.exp(s0 - top0_ref[0, head, :]) * zinv_ref[head, :]
        e1_ref[rows, :] = jnp.exp(s1 - top1_ref[0, head, :]).astype(e1_ref.dtype)
        rankb_out_ref[rows, :] = rankb_ref[rows, :].astype(rankb_out_ref.dtype)


def _peer_topk(q, keys, tb):
    t = q.shape[0]
    rows = PEER_HEADS * PEER_N_KEYS
    out_w = jax.ShapeDtypeStruct((PEER_HEADS, PEER_N_KEYS, t), F32)
    out_b = jax.ShapeDtypeStruct((rows, t), BF16)
    ospec = pl.BlockSpec((rows, tb), lambda i: (0, i))
    wspec = pl.BlockSpec((PEER_HEADS, PEER_N_KEYS, tb), lambda i: (0, 0, i))
    hv = (PEER_HEADS, tb)
    return pl.pallas_call(
        _peer_topk_kernel,
        grid=(t // tb,),
        in_specs=[pl.BlockSpec((tb, q.shape[1]), lambda i: (i, 0)),
                  pl.BlockSpec(keys.shape, lambda i: (0, 0, 0))],
        out_specs=[ospec, ospec, wspec, wspec],
        out_shape=[out_b, out_b, out_w, out_w],
        scratch_shapes=[pltpu.VMEM((2 * rows, tb), F32),
                        pltpu.VMEM((rows, tb), F32),
                        pltpu.VMEM((rows, tb), F32),
                        pltpu.VMEM((PEER_TOPK,) + hv, F32),
                        pltpu.VMEM((PEER_TOPK,) + hv, F32),
                        pltpu.VMEM((len(_CANDS),) + hv, F32),
                        pltpu.VMEM((PEER_TOPK,) + hv, F32),
                        pltpu.VMEM(hv, F32)],
        compiler_params=_cparams("parallel"),
        name="peer_topk",
    )(q, keys)


def _peer_expert_kernel(xn_ref, u_ref, vt_ref, rankb_ref, e1_ref, nsel_ref, coef_ref,
                        xres_ref, gf_ref, y_ref, acc_ref, *, n_chunks, i_per_chunk, ahead,
                        down_group, final_norm):
    et = pl.program_id(1)
    nk = PEER_N_KEYS
    ck = i_per_chunk * nk

    @pl.when(et == 0)
    def _():
        acc_ref[...] = jnp.zeros_like(acc_ref)

    def rows_of(row_ref, h, il):
        row = row_ref[h, pl.ds(il, 1), :].astype(BF16)
        return jnp.broadcast_to(row, (nk, row.shape[-1]))

    def up_proj(c):
        return lax.dot_general(u_ref[c * ck:(c + 1) * ck, :], xn_ref[...], (((1,), (1,)), ((), ())),
                               preferred_element_type=F32)

    a_next = [up_proj(c) for c in range(min(ahead, n_chunks))]
    zero = jnp.zeros((), BF16)
    hidden = []
    for c in range(n_chunks):
        if c + ahead < n_chunks:
            a_next.append(up_proj(c + ahead))
        a_t = a_next[c]
        for ic in range(i_per_chunk):
            il = c * i_per_chunk + ic
            gate = None
            for h in range(PEER_HEADS):
                hs = slice(h * nk, (h + 1) * nk)
                sel = jnp.where(rankb_ref[hs, :] < rows_of(nsel_ref, h, il), e1_ref[hs, :], zero)
                sel = sel * rows_of(coef_ref, h, il)
                gate = sel if gate is None else gate + sel
            act = _gelu_tanh(a_t[ic * nk:(ic + 1) * nk, :]).astype(BF16)
            hidden.append(gate * act)
        if (c + 1) % down_group == 0:
            rows = slice((c + 1 - down_group) * ck, (c + 1) * ck)
            acc_ref[...] += jnp.dot(vt_ref[:, rows], jnp.concatenate(hidden, axis=0),
                                    preferred_element_type=F32)
            hidden = []

    @pl.when(et == pl.num_programs(1) - 1)
    def _():
        y = xres_ref[...] + acc_ref[...].T
        if final_norm:
            y = _rms(y, gf_ref[...])
        y_ref[...] = y


def _peer_experts(xn, u, v_t, rankb, e1, nsel, coef, xres, g_final, tb, e_tile, i_per_chunk,
                  ahead, down_group, final_norm):
    t, d = xn.shape
    rows = PEER_HEADS * PEER_N_KEYS
    aux = pl.BlockSpec((rows, tb), lambda i, j: (0, i))
    per_tile = pl.BlockSpec((PEER_HEADS, e_tile // PEER_N_KEYS, tb), lambda i, j: (0, j, i))
    return pl.pallas_call(
        functools.partial(_peer_expert_kernel, n_chunks=e_tile // (i_per_chunk * PEER_N_KEYS),
                          i_per_chunk=i_per_chunk, ahead=ahead, down_group=down_group,
                          final_norm=final_norm),
        grid=(t // tb, u.shape[0] // e_tile),
        in_specs=[pl.BlockSpec((tb, d), lambda i, j: (i, 0)),
                  pl.BlockSpec((e_tile, d), lambda i, j: (j, 0)),
                  pl.BlockSpec((d, e_tile), lambda i, j: (0, j)),
                  aux, aux, per_tile, per_tile,
                  pl.BlockSpec((tb, d), lambda i, j: (i, 0)),
                  pl.BlockSpec((1, d), lambda i, j: (0, 0))],
        out_specs=pl.BlockSpec((tb, d), lambda i, j: (i, 0)),
        out_shape=jax.ShapeDtypeStruct((t, d), F32),
        scratch_shapes=[pltpu.VMEM((d, tb), F32)],
        compiler_params=_cparams("parallel", "arbitrary"),
        name="peer_experts",
    )(xn, u, v_t, rankb, e1, nsel, coef, xres, g_final.reshape(1, d))


def _row_tile(t, pref):
    return pref if t % pref == 0 else t


def _in_projection(x2, g_mix, w_in, sgu_g):
    t = x2.shape[0]
    xn = _norm_cast(x2, g_mix, _row_tile(t, 512))
    tm = _row_tile(t, 1024)
    c0, c1, c2, c3, c4 = D_ATTN, D_ATTN + 2 * D_KV, D_ATTN + 2 * D_KV + D_SGU, \
        D_ATTN + 2 * D_KV + 2 * D_SGU, w_in.shape[1]
    q = _mm(xn, w_in[:, :c0], "scale_q", BF16, tm, 1024, name="proj_q")
    kv = _mm(xn, w_in[:, c0:c1], "none", F32, tm, 2 * D_KV, name="proj_kv")
    u = _mm(xn, w_in[:, c1:c2], "gelu", BF16, tm, 1024, name="proj_u")
    vs = _mm(xn, w_in[:, c2:c3], "gelu_norm", F32, _row_tile(t, 512), D_SGU, g=sgu_g, name="proj_vsgu")
    gates = _mm(xn, w_in[:, c3:c4], "sigmoid", BF16, tm, 1024, name="proj_gates")
    return q, kv, u, vs, gates


def _peer(xn, xres, w_query, keys, u, v_t, g_final, final_norm):
    t = xn.shape[0]
    q = _mm(xn, w_query, "none", BF16, _row_tile(t, 1024), 1024, name="peer_query")
    rankb, e1, nsel, coef = _peer_topk(q, keys, LANES)
    tb = _row_tile(t, 512)
    return _peer_experts(xn, u, v_t, rankb, e1, nsel, coef, xres, g_final, tb, 1024, 2, 2, 1,
                         final_norm)


def kernel(x_prompt, x_sample, cache_k_swa, cache_v_swa, norm_mix_g, w_in, sgu_norm_g, sgu_w_s,
           sgu_b_s, attn_sinks, rel_bias_table, w_branch_attn, w_branch_sgu, w_out, norm_ffn_g,
           peer_w_query, peer_sub_keys, peer_expert_u, peer_expert_v, norm_final_g):
    B, S, D = x_prompt.shape
    Bd, T, _ = x_sample.shape
    depth = w_in.shape[0]
    bias_p = _rel_bias(rel_bias_table, CHUNK, WINDOW + CHUNK)
    bias_s = _rel_bias(rel_bias_table, T, WINDOW + T)

    xp = x_prompt.reshape(B * S, D)
    xs = x_sample.reshape(Bd * T, D)
    nk_p, nv_p, nk_s, nv_s, nsgu_s = [], [], [], [], []
    for l in range(depth):
        last = l == depth - 1
        w_in_l = w_in[l].astype(BF16)
        w_pa = w_branch_attn[l].astype(BF16)
        w_pb = w_branch_sgu[l].astype(BF16)
        w_o = w_out[l].astype(BF16)
        w_q = peer_w_query[l].astype(BF16)
        keys = peer_sub_keys[l].astype(BF16).reshape(2 * PEER_HEADS, PEER_N_KEYS, PEER_HALF)
        e_u = peer_expert_u[l].astype(BF16)
        e_vt = peer_expert_v[l].astype(BF16).T
        g_out = norm_final_g if last else jnp.ones_like(norm_final_g)

        q, kv, u, vs, gates = _in_projection(xp, norm_mix_g[l], w_in_l, sgu_norm_g[l])
        kv3 = kv.reshape(B, S, 2 * D_KV)
        kpad = jnp.pad(kv3[..., :D_KV].astype(BF16), ((0, 0), (WINDOW, 0), (0, 0)))
        vpad = jnp.pad(kv3[..., D_KV:].astype(BF16), ((0, 0), (WINDOW, 0), (0, 0)))
        o = _attention(q.reshape(B, S, D_ATTN), kpad, vpad, bias_p, attn_sinks[l],
                       CHUNK, WINDOW + CHUNK, WINDOW)
        n_blk = S // SGU_BLOCK
        sg = _sgu(u.reshape(B * n_blk, SGU_BLOCK, D_SGU), vs.reshape(B * n_blk, SGU_BLOCK, D_SGU),
                  sgu_w_s[l], sgu_b_s[l], SGU_BLOCK)
        xp, xnp = _merge(o.reshape(B * S, D_ATTN), sg.reshape(B * S, D_SGU), gates, xp,
                         w_pa, w_pb, w_o, norm_ffn_g[l], _row_tile(B * S, 256))
        nk_p.append(kv3[:, S - WINDOW:, :D_KV].reshape(B, WINDOW, N_KV_HEADS, HEAD_DIM))
        nv_p.append(kv3[:, S - WINDOW:, D_KV:].reshape(B, WINDOW, N_KV_HEADS, HEAD_DIM))

        qs, kvs, us, vss, gates_s = _in_projection(xs, norm_mix_g[l], w_in_l, sgu_norm_g[l])
        kvs3 = kvs.reshape(Bd, T, 2 * D_KV)
        k_all = jnp.concatenate([cache_k_swa[l].reshape(Bd, WINDOW, D_KV).astype(BF16),
                                 kvs3[..., :D_KV].astype(BF16)], axis=1)
        v_all = jnp.concatenate([cache_v_swa[l].reshape(Bd, WINDOW, D_KV).astype(BF16),
                                 kvs3[..., D_KV:].astype(BF16)], axis=1)
        os_ = _attention(qs.reshape(Bd, T, D_ATTN), k_all, v_all, bias_s, attn_sinks[l],
                         T, WINDOW + T, 0)
        ss = _sgu(us.reshape(Bd, T, D_SGU), vss.reshape(Bd, T, D_SGU), sgu_w_s[l], sgu_b_s[l], T)
        xs, xns = _merge(os_.reshape(Bd * T, D_ATTN), ss.reshape(Bd * T, D_SGU), gates_s, xs,
                         w_pa, w_pb, w_o, norm_ffn_g[l], Bd * T)
        nk_s.append(kvs3[..., :D_KV].reshape(Bd, T, N_KV_HEADS, HEAD_DIM))
        nv_s.append(kvs3[..., D_KV:].reshape(Bd, T, N_KV_HEADS, HEAD_DIM))
        nsgu_s.append(vss.reshape(Bd, T, D_SGU))

        xp = _peer(xnp, xp, w_q, keys, e_u, e_vt, g_out, last)
        xs = _peer(xns, xs, w_q, keys, e_u, e_vt, g_out, last)

    return (xp.reshape(B, S, D), xs.reshape(Bd, T, D), jnp.stack(nk_p), jnp.stack(nv_p),
            jnp.stack(nk_s), jnp.stack(nv_s), jnp.stack(nsgu_s))
```

```python
import functools
import math

import jax
import jax.numpy as jnp
import numpy as np
from jax import lax
from jax.experimental import pallas as pl
from jax.experimental.pallas import tpu as pltpu

F32 = jnp.float32
BF16 = jnp.bfloat16

CHUNK = 64
WINDOW = 128
N_HEADS = 16
N_KV_HEADS = 2
HEAD_DIM = 64
GQA_GROUP = N_HEADS // N_KV_HEADS
D_ATTN = N_HEADS * HEAD_DIM
D_KV = N_KV_HEADS * HEAD_DIM
N_BUCKETS = 32
MAX_DISTANCE = 128
SGU_BLOCK = 128
SGU_GROUPS = 8
SGU_GROUP_DIM = 128
D_SGU = SGU_GROUPS * SGU_GROUP_DIM
PEER_HEADS = 8
PEER_N_KEYS = 128
PEER_HALF = 128
PEER_TOPK = 16
EPS = 1e-6
NEG_INF = -1e30

LANES = 128
SUBLANES = 8
VMEM_LIMIT_BYTES = 56 * 1024 * 1024

_CANDS = tuple((k, l) for k in range(PEER_TOPK) for l in range(PEER_TOPK)
               if (k + 1) * (l + 1) <= PEER_TOPK)


def _cparams(*sem):
    return pltpu.CompilerParams(dimension_semantics=sem, vmem_limit_bytes=VMEM_LIMIT_BYTES)


def _gelu_tanh(x):
    c = math.sqrt(2.0 / math.pi)
    t = jnp.tanh(x * (c + (c * 0.044715) * (x * x)))
    hx = 0.5 * x
    return hx + hx * t


def _rms(x, g):
    r = lax.rsqrt(jnp.mean(x * x, axis=-1, keepdims=True) + EPS)
    return x * r * g


def _norm_q_kernel(x_ref, g_ref, w_ref, xn_ref, q_ref):
    xn = _rms(x_ref[...], g_ref[...]).astype(xn_ref.dtype)
    xn_ref[...] = xn
    q = jnp.dot(xn, w_ref[...], preferred_element_type=F32)
    q_ref[...] = (q * (HEAD_DIM ** -0.5)).astype(q_ref.dtype)


def _norm_and_q(x, g, w_q, tm):
    t, d = x.shape
    n = w_q.shape[1]
    return pl.pallas_call(
        _norm_q_kernel,
        grid=(t // tm,),
        in_specs=[pl.BlockSpec((tm, d), lambda i: (i, 0)),
                  pl.BlockSpec((1, d), lambda i: (0, 0)),
                  pl.BlockSpec((d, n), lambda i: (0, 0), pipeline_mode=pl.Buffered(1))],
        out_specs=[pl.BlockSpec((tm, d), lambda i: (i, 0)),
                   pl.BlockSpec((tm, n), lambda i: (i, 0))],
        out_shape=[jax.ShapeDtypeStruct((t, d), BF16),
                   jax.ShapeDtypeStruct((t, n), BF16)],
        compiler_params=_cparams("parallel"),
        name="norm_proj_q",
    )(x, g.reshape(1, d), w_q)


def _mm_kernel(x_ref, w_ref, *rest, epilogue):
    acc = jnp.dot(x_ref[...], w_ref[...], preferred_element_type=F32)
    if epilogue == "none":
        (o_ref,) = rest
        o_ref[...] = acc.astype(o_ref.dtype)
    elif epilogue == "gelu":
        (o_ref,) = rest
        o_ref[...] = jax.nn.gelu(acc).astype(o_ref.dtype)
    elif epilogue == "gelu_norm":
        g_ref, o_ref = rest
        o_ref[...] = _rms(jax.nn.gelu(acc), g_ref[...]).astype(o_ref.dtype)
    elif epilogue == "sigmoid":
        (o_ref,) = rest
        o_ref[...] = jax.nn.sigmoid(acc).astype(o_ref.dtype)
    else:
        raise ValueError(epilogue)


def _mm(x, w, epilogue, out_dtype, tm, tn, g=None, name="mm"):
    t, k = x.shape
    n = w.shape[1]
    in_specs = [pl.BlockSpec((tm, k), lambda i, j: (i, 0)),
                pl.BlockSpec((k, tn), lambda i, j: (0, j))]
    args = [x, w]
    if g is not None:
        in_specs.append(pl.BlockSpec((1, tn), lambda i, j: (0, j)))
        args.append(g.reshape(1, n))
    return pl.pallas_call(
        functools.partial(_mm_kernel, epilogue=epilogue),
        grid=(t // tm, n // tn),
        in_specs=in_specs,
        out_specs=pl.BlockSpec((tm, tn), lambda i, j: (i, j)),
        out_shape=jax.ShapeDtypeStruct((t, n), out_dtype),
        compiler_params=_cparams("parallel", "parallel"),
        name=name,
    )(*args)


def _bias_kernel(table_ref, bucket_ref, o_ref):
    bucket = bucket_ref[...]
    for h in range(N_HEADS):
        acc = jnp.zeros(bucket.shape, F32)
        for b in range(N_BUCKETS):
            acc = jnp.where(bucket == b, table_ref[b, h], acc)
        o_ref[h] = acc


def _t5_bucket(rel):
    half = N_BUCKETS // 2
    max_exact = half // 2
    ret = jnp.where(rel > 0, half, 0)
    n = jnp.abs(rel)
    nf = jnp.maximum(n, 1).astype(jnp.float32)
    large = max_exact + (jnp.log(nf / max_exact) / math.log(MAX_DISTANCE / max_exact)
                         * (half - max_exact)).astype(jnp.int32)
    large = jnp.minimum(large, half - 1)
    return ret + jnp.where(n < max_exact, n, large)


def _rel_bias(table, n_q, n_k):
    rel = (jnp.arange(n_k, dtype=jnp.int32)[None, :] - WINDOW
           - jnp.arange(n_q, dtype=jnp.int32)[:, None])
    bucket = _t5_bucket(rel).astype(jnp.int32) & (N_BUCKETS - 1)
    return pl.pallas_call(
        _bias_kernel,
        in_specs=[pl.BlockSpec(memory_space=pltpu.SMEM),
                  pl.BlockSpec(memory_space=pltpu.VMEM)],
        out_specs=pl.BlockSpec(memory_space=pltpu.VMEM),
        out_shape=jax.ShapeDtypeStruct((N_HEADS, n_q, n_k), F32),
        name="rel_bias",
    )(table.astype(F32), bucket)


def _attn_kernel(sinks_ref, q_ref, k_ref, v_ref, bias_ref, o_ref, *, n_q, n_k, n_pad, n_sub):
    heads = range(N_HEADS)
    kv_cols = [slice((h // GQA_GROUP) * HEAD_DIM, (h // GQA_GROUP + 1) * HEAD_DIM) for h in heads]
    for sub in range(n_sub):
        c = pl.program_id(1) * n_sub + sub
        start = pl.multiple_of(c * n_q, n_q)
        q_rows = slice(sub * n_q, (sub + 1) * n_q)
        kb = k_ref[0, pl.ds(start, n_k), :]
        vb = v_ref[0, pl.ds(start, n_k), :]
        key_row = start + lax.broadcasted_iota(jnp.int32, (1, n_k), 1)
        valid = key_row >= n_pad
        s = [lax.dot_general(q_ref[0, q_rows, h * HEAD_DIM:(h + 1) * HEAD_DIM], kb[:, kv_cols[h]],
                             (((1,), (1,)), ((), ())), preferred_element_type=F32) for h in heads]
        s = [jnp.where(valid, s[h] + bias_ref[h], NEG_INF) for h in heads]
        m = [jnp.maximum(jnp.max(s[h], axis=-1, keepdims=True), sinks_ref[h]) for h in heads]
        p = [jnp.exp(s[h] - m[h]) for h in heads]
        denom = [jnp.sum(p[h], axis=-1, keepdims=True) + jnp.exp(sinks_ref[h] - m[h])
                 for h in heads]
        o = [jnp.dot(p[h].astype(BF16), vb[:, kv_cols[h]], preferred_element_type=F32)
             for h in heads]
        for h in heads:
            o_ref[0, q_rows, h * HEAD_DIM:(h + 1) * HEAD_DIM] = (o[h] / denom[h]).astype(o_ref.dtype)


def _attention(q, k, v, bias, sinks, n_q, n_k, n_pad):
    nb, sq, _ = q.shape
    rows = k.shape[1]
    n_sub = 2 if (sq // n_q) % 2 == 0 else 1
    return pl.pallas_call(
        functools.partial(_attn_kernel, n_q=n_q, n_k=n_k, n_pad=n_pad, n_sub=n_sub),
        grid=(nb, sq // (n_q * n_sub)),
        in_specs=[pl.BlockSpec(memory_space=pltpu.SMEM),
                  pl.BlockSpec((1, n_q * n_sub, D_ATTN), lambda b, c: (b, c, 0)),
                  pl.BlockSpec((1, rows, D_KV), lambda b, c: (b, 0, 0)),
                  pl.BlockSpec((1, rows, D_KV), lambda b, c: (b, 0, 0)),
                  pl.BlockSpec((N_HEADS, n_q, n_k), lambda b, c: (0, 0, 0))],
        out_specs=pl.BlockSpec((1, n_q * n_sub, D_ATTN), lambda b, c: (b, c, 0)),
        out_shape=jax.ShapeDtypeStruct((nb, sq, D_ATTN), BF16),
        compiler_params=_cparams("parallel", "arbitrary"),
        name="swa_attention",
    )(sinks.astype(F32), q, k, v, bias)


def _sgu_kernel(u_ref, v_ref, w_ref, b_ref, o_ref, *, rows):
    r = lax.broadcasted_iota(jnp.int32, (rows, rows), 0)
    c = lax.broadcasted_iota(jnp.int32, (rows, rows), 1)
    tril = c <= r
    for g in range(SGU_GROUPS):
        sl = slice(g * SGU_GROUP_DIM, (g + 1) * SGU_GROUP_DIM)
        w = jnp.where(tril, w_ref[g], 0.0).astype(BF16)
        mixed = jnp.dot(w, v_ref[0, :, sl].astype(BF16), preferred_element_type=F32)
        o_ref[0, :, sl] = (u_ref[0, :, sl].astype(F32) * (mixed + b_ref[:, sl])).astype(o_ref.dtype)


def _sgu(u, vn, w_s, b_s, rows):
    nblk = u.shape[0]
    w = w_s[:, :rows, :rows]
    b_full = jnp.repeat(b_s[:, :rows].T, SGU_GROUP_DIM, axis=1).astype(F32)
    return pl.pallas_call(
        functools.partial(_sgu_kernel, rows=rows),
        grid=(nblk,),
        in_specs=[pl.BlockSpec((1, rows, D_SGU), lambda i: (i, 0, 0)),
                  pl.BlockSpec((1, rows, D_SGU), lambda i: (i, 0, 0)),
                  pl.BlockSpec((SGU_GROUPS, rows, rows), lambda i: (0, 0, 0)),
                  pl.BlockSpec((rows, D_SGU), lambda i: (0, 0))],
        out_specs=pl.BlockSpec((1, rows, D_SGU), lambda i: (i, 0, 0)),
        out_shape=jax.ShapeDtypeStruct((nblk, rows, D_SGU), BF16),
        compiler_params=_cparams("parallel"),
        name="sgu",
    )(u, vn, w, b_full)


def _merge_kernel(o_ref, s_ref, gate_ref, x_ref, wpa_ref, wpb_ref, wout_ref, gn_ref,
                  xo_ref, xn_ref):
    d = x_ref.shape[-1]
    ta = jnp.dot(o_ref[...], wpa_ref[...], preferred_element_type=F32)
    tb = jnp.dot(s_ref[...], wpb_ref[...], preferred_element_type=F32)
    h = gate_ref[:, :d].astype(F32) * ta + gate_ref[:, d:].astype(F32) * tb
    y = x_ref[...] + jnp.dot(h.astype(BF16), wout_ref[...], preferred_element_type=F32)
    xo_ref[...] = y
    xn_ref[...] = _rms(y, gn_ref[...]).astype(xn_ref.dtype)


def _merge(o, s, gates, x, w_pa, w_pb, w_out, g_ffn, tm):
    t, d = x.shape
    once = pl.Buffered(1)
    return pl.pallas_call(
        _merge_kernel,
        grid=(t // tm,),
        in_specs=[pl.BlockSpec((tm, D_ATTN), lambda i: (i, 0)),
                  pl.BlockSpec((tm, D_SGU), lambda i: (i, 0)),
                  pl.BlockSpec((tm, 2 * d), lambda i: (i, 0)),
                  pl.BlockSpec((tm, d), lambda i: (i, 0)),
                  pl.BlockSpec((D_ATTN, d), lambda i: (0, 0), pipeline_mode=once),
                  pl.BlockSpec((D_SGU, d), lambda i: (0, 0), pipeline_mode=once),
                  pl.BlockSpec((d, d), lambda i: (0, 0), pipeline_mode=once),
                  pl.BlockSpec((1, d), lambda i: (0, 0))],
        out_specs=[pl.BlockSpec((tm, d), lambda i: (i, 0)),
                   pl.BlockSpec((tm, d), lambda i: (i, 0))],
        out_shape=[jax.ShapeDtypeStruct((t, d), F32),
                   jax.ShapeDtypeStruct((t, d), BF16)],
        compiler_params=_cparams("parallel"),
        name="merge",
    )(o, s, gates, x, w_pa, w_pb, w_out, g_ffn.reshape(1, d))


def _peer_topk_kernel(q_ref, keys_ref, rankb_out_ref, e1_ref, nsel_ref, coef_ref,
                      s_ref, ranka_ref, rankb_ref, top0_ref, top1_ref, cand_ref, cnt_ref, zinv_ref):
    nk = PEER_N_KEYS
    tb = q_ref.shape[0]
    q = q_ref[...]
    for hp in range(2 * PEER_HEADS):
        s_ref[hp * nk:(hp + 1) * nk, :] = lax.dot_general(
            keys_ref[hp], q[:, hp * PEER_HALF:(hp + 1) * PEER_HALF],
            (((1,), (1,)), ((), ())), preferred_element_type=F32)

    key_id = lax.broadcasted_iota(jnp.int32, (nk, tb), 0).astype(F32)
    heads_per_loop = 2
    for h0 in range(0, PEER_HEADS, heads_per_loop):
        heads = range(h0, h0 + heads_per_loop)
        for h in heads:
            rows = slice(h * nk, (h + 1) * nk)
            ranka_ref[rows, :] = jnp.full((nk, tb), float(PEER_TOPK), F32)
            rankb_ref[rows, :] = jnp.full((nk, tb), float(PEER_TOPK), F32)

        def body(r, works, heads=heads):
            rf = jnp.asarray(r, F32)
            out = []
            chains = [(h, rank_ref, top_ref) for h in heads
                      for rank_ref, top_ref in ((ranka_ref, top0_ref), (rankb_ref, top1_ref))]
            for w, (h, rank_ref, top_ref) in zip(works, chains):
                rows = slice(h * nk, (h + 1) * nk)
                m = jnp.max(w, axis=0, keepdims=True)
                first = jnp.min(jnp.where(w == m, key_id, float(nk)), axis=0, keepdims=True)
                hit = key_id == first
                out.append(jnp.where(hit, -jnp.inf, w))
                rank_ref[rows, :] = jnp.where(hit, rf, rank_ref[rows, :])
                top_ref[r, pl.ds(h, 1), :] = m
            return tuple(out)

        lax.fori_loop(0, PEER_TOPK, body,
                      tuple(s_ref[hp * nk:(hp + 1) * nk, :]
                            for hp in range(2 * h0, 2 * (h0 + heads_per_loop))))

    for c, (k, l) in enumerate(_CANDS):
        cand_ref[c] = top0_ref[k] + top1_ref[l]
    cand_at = {kl: c for c, kl in enumerate(_CANDS)}
    row_len = [sum(1 for (k2, _) in _CANDS if k2 == k) for k in range(PEER_TOPK)]
    cv0 = cand_ref[cand_at[(0, 0)]]
    zeros = jnp.zeros(cv0.shape, F32)

    def body(r, carry):
        front, taken, z = carry
        m = functools.reduce(jnp.maximum, front)
        first = functools.reduce(
            jnp.minimum, [jnp.where(front[k] == m, float(k), float(PEER_TOPK))
                          for k in range(PEER_TOPK)])
        new_front, new_taken = [], []
        for k in range(PEER_TOPK):
            hit = first == float(k)
            n_k = taken[k] + jnp.where(hit, 1.0, 0.0)
            nxt = jnp.full(cv0.shape, -jnp.inf, F32)
            for l in range(1, row_len[k]):
                nxt = jnp.where(n_k == float(l), cand_ref[cand_at[(k, l)]], nxt)
            new_front.append(jnp.where(hit, nxt, front[k]))
            new_taken.append(n_k)
        return tuple(new_front), tuple(new_taken), z + jnp.exp(m - cv0)

    front0 = tuple(cand_ref[cand_at[(k, 0)]] for k in range(PEER_TOPK))
    _, taken, z = lax.fori_loop(0, PEER_TOPK, body, (front0, (zeros,) * PEER_TOPK, zeros))
    for k in range(PEER_TOPK):
        cnt_ref[k] = taken[k]
    zinv_ref[...] = 1.0 / z
    for h in range(PEER_HEADS):
        rows = slice(h * nk, (h + 1) * nk)
        head = pl.ds(h, 1)
        ra = ranka_ref[rows, :]
        nsel = jnp.zeros((nk, tb), F32)
        for k in range(PEER_TOPK):
            nsel = jnp.where(ra == float(k), cnt_ref[k, head, :], nsel)
        nsel_ref[h] = nsel
        s0 = s_ref[2 * h * nk:(2 * h + 1) * nk, :]
        s1 = s_ref[(2 * h + 1) * nk:(2 * h + 2) * nk, :]
        coef_ref[h] = jnp.exp(s0 - top0_ref[0, head, :]) * zinv_ref[head, :]
        e1_ref[rows, :] = jnp.exp(s1 - top1_ref[0, head, :]).astype(e1_ref.dtype)
        rankb_out_ref[rows, :] = rankb_ref[rows, :].astype(rankb_out_ref.dtype)


def _peer_topk(q, keys, tb):
    t = q.shape[0]
    rows = PEER_HEADS * PEER_N_KEYS
    out_w = jax.ShapeDtypeStruct((PEER_HEADS, PEER_N_KEYS, t), F32)
    out_b = jax.ShapeDtypeStruct((rows, t), BF16)
    ospec = pl.BlockSpec((rows, tb), lambda i: (0, i))
    wspec = pl.BlockSpec((PEER_HEADS, PEER_N_KEYS, tb), lambda i: (0, 0, i))
    hv = (PEER_HEADS, tb)
    return pl.pallas_call(
        _peer_topk_kernel,
        grid=(t // tb,),
        in_specs=[pl.BlockSpec((tb, q.shape[1]), lambda i: (i, 0)),
                  pl.BlockSpec(keys.shape, lambda i: (0, 0, 0))],
        out_specs=[ospec, ospec, wspec, wspec],
        out_shape=[out_b, out_b, out_w, out_w],
        scratch_shapes=[pltpu.VMEM((2 * rows, tb), F32),
                        pltpu.VMEM((rows, tb), F32),
                        pltpu.VMEM((rows, tb), F32),
                        pltpu.VMEM((PEER_TOPK,) + hv, F32),
                        pltpu.VMEM((PEER_TOPK,) + hv, F32),
                        pltpu.VMEM((len(_CANDS),) + hv, F32),
                        pltpu.VMEM((PEER_TOPK,) + hv, F32),
                        pltpu.VMEM(hv, F32)],
        compiler_params=_cparams("parallel"),
        name="peer_topk",
    )(q, keys)


def _peer_expert_kernel(xn_ref, u_ref, vt_ref, rankb_ref, e1_ref, nsel_ref, coef_ref,
                        xres_ref, gf_ref, y_ref, acc_ref, *, n_chunks, i_per_chunk, ahead,
                        down_group, final_norm):
    et = pl.program_id(1)
    nk = PEER_N_KEYS
    ck = i_per_chunk * nk

    @pl.when(et == 0)
    def _():
        acc_ref[...] = jnp.zeros_like(acc_ref)

    def rows_of(row_ref, h, il):
        row = row_ref[h, pl.ds(il, 1), :].astype(BF16)
        return jnp.broadcast_to(row, (nk, row.shape[-1]))

    def up_proj(c):
        return lax.dot_general(u_ref[c * ck:(c + 1) * ck, :], xn_ref[...], (((1,), (1,)), ((), ())),
                               preferred_element_type=F32)

    a_next = [up_proj(c) for c in range(min(ahead, n_chunks))]
    zero = jnp.zeros((), BF16)
    hidden = []
    for c in range(n_chunks):
        if c + ahead < n_chunks:
            a_next.append(up_proj(c + ahead))
        a_t = a_next[c]
        for ic in range(i_per_chunk):
            il = c * i_per_chunk + ic
            gate = None
            for h in range(PEER_HEADS):
                hs = slice(h * nk, (h + 1) * nk)
                sel = jnp.where(rankb_ref[hs, :] < rows_of(nsel_ref, h, il), e1_ref[hs, :], zero)
                sel = sel * rows_of(coef_ref, h, il)
                gate = sel if gate is None else gate + sel
            act = _gelu_tanh(a_t[ic * nk:(ic + 1) * nk, :]).astype(BF16)
            hidden.append(gate * act)
        if (c + 1) % down_group == 0:
            rows = slice((c + 1 - down_group) * ck, (c + 1) * ck)
            acc_ref[...] += jnp.dot(vt_ref[:, rows], jnp.concatenate(hidden, axis=0),
                                    preferred_element_type=F32)
            hidden = []

    @pl.when(et == pl.num_programs(1) - 1)
    def _():
        y = xres_ref[...] + acc_ref[...].T
        if final_norm:
            y = _rms(y, gf_ref[...])
        y_ref[...] = y


def _peer_experts(xn, u, v_t, rankb, e1, nsel, coef, xres, g_final, tb, e_tile, i_per_chunk,
                  ahead, down_group, final_norm):
    t, d = xn.shape
    rows = PEER_HEADS * PEER_N_KEYS
    aux = pl.BlockSpec((rows, tb), lambda i, j: (0, i))
    per_tile = pl.BlockSpec((PEER_HEADS, e_tile // PEER_N_KEYS, tb), lambda i, j: (0, j, i))
    return pl.pallas_call(
        functools.partial(_peer_expert_kernel, n_chunks=e_tile // (i_per_chunk * PEER_N_KEYS),
                          i_per_chunk=i_per_chunk, ahead=ahead, down_group=down_group,
                          final_norm=final_norm),
        grid=(t // tb, u.shape[0] // e_tile),
        in_specs=[pl.BlockSpec((tb, d), lambda i, j: (i, 0)),
                  pl.BlockSpec((e_tile, d), lambda i, j: (j, 0)),
                  pl.BlockSpec((d, e_tile), lambda i, j: (0, j)),
                  aux, aux, per_tile, per_tile,
                  pl.BlockSpec((tb, d), lambda i, j: (i, 0)),
                  pl.BlockSpec((1, d), lambda i, j: (0, 0))],
        out_specs=pl.BlockSpec((tb, d), lambda i, j: (i, 0)),
        out_shape=jax.ShapeDtypeStruct((t, d), F32),
        scratch_shapes=[pltpu.VMEM((d, tb), F32)],
        compiler_params=_cparams("parallel", "arbitrary"),
        name="peer_experts",
    )(xn, u, v_t, rankb, e1, nsel, coef, xres, g_final.reshape(1, d))


def _row_tile(t, pref):
    return pref if t % pref == 0 else t


def _in_projection(x2, g_mix, w_in, sgu_g):
    t = x2.shape[0]
    tm = _row_tile(t, 1024)
    c0, c1, c2, c3, c4 = D_ATTN, D_ATTN + 2 * D_KV, D_ATTN + 2 * D_KV + D_SGU, \
        D_ATTN + 2 * D_KV + 2 * D_SGU, w_in.shape[1]
    xn, q = _norm_and_q(x2, g_mix, w_in[:, :c0], _row_tile(t, 512))
    kv = _mm(xn, w_in[:, c0:c1], "none", F32, tm, 2 * D_KV, name="proj_kv")
    u = _mm(xn, w_in[:, c1:c2], "gelu", BF16, tm, 1024, name="proj_u")
    vs = _mm(xn, w_in[:, c2:c3], "gelu_norm", F32, _row_tile(t, 512), D_SGU, g=sgu_g, name="proj_vsgu")
    gates = _mm(xn, w_in[:, c3:c4], "sigmoid", BF16, tm, 1024, name="proj_gates")
    return q, kv, u, vs, gates


def _peer(xn, xres, w_query, keys, u, v_t, g_final, final_norm):
    t = xn.shape[0]
    q = _mm(xn, w_query, "none", BF16, _row_tile(t, 1024), 1024, name="peer_query")
    rankb, e1, nsel, coef = _peer_topk(q, keys, LANES)
    tb = _row_tile(t, 512)
    return _peer_experts(xn, u, v_t, rankb, e1, nsel, coef, xres, g_final, tb, 1024, 2, 2, 1,
                         final_norm)


def kernel(x_prompt, x_sample, cache_k_swa, cache_v_swa, norm_mix_g, w_in, sgu_norm_g, sgu_w_s,
           sgu_b_s, attn_sinks, rel_bias_table, w_branch_attn, w_branch_sgu, w_out, norm_ffn_g,
           peer_w_query, peer_sub_keys, peer_expert_u, peer_expert_v, norm_final_g):
    B, S, D = x_prompt.shape
    Bd, T, _ = x_sample.shape
    depth = w_in.shape[0]
    bias_p = _rel_bias(rel_bias_table, CHUNK, WINDOW + CHUNK)
    bias_s = _rel_bias(rel_bias_table, T, WINDOW + T)

    xp = x_prompt.reshape(B * S, D)
    xs = x_sample.reshape(Bd * T, D)
    nk_p, nv_p, nk_s, nv_s, nsgu_s = [], [], [], [], []
    for l in range(depth):
        last = l == depth - 1
        w_in_l = w_in[l].astype(BF16)
        w_pa = w_branch_attn[l].astype(BF16)
        w_pb = w_branch_sgu[l].astype(BF16)
        w_o = w_out[l].astype(BF16)
        w_q = peer_w_query[l].astype(BF16)
        keys = peer_sub_keys[l].astype(BF16).reshape(2 * PEER_HEADS, PEER_N_KEYS, PEER_HALF)
        e_u = peer_expert_u[l].astype(BF16)
        e_vt = peer_expert_v[l].astype(BF16).T
        g_out = norm_final_g if last else jnp.ones_like(norm_final_g)

        q, kv, u, vs, gates = _in_projection(xp, norm_mix_g[l], w_in_l, sgu_norm_g[l])
        kv3 = kv.reshape(B, S, 2 * D_KV)
        kpad = jnp.pad(kv3[..., :D_KV].astype(BF16), ((0, 0), (WINDOW, 0), (0, 0)))
        vpad = jnp.pad(kv3[..., D_KV:].astype(BF16), ((0, 0), (WINDOW, 0), (0, 0)))
        o = _attention(q.reshape(B, S, D_ATTN), kpad, vpad, bias_p, attn_sinks[l],
                       CHUNK, WINDOW + CHUNK, WINDOW)
        n_blk = S // SGU_BLOCK
        sg = _sgu(u.reshape(B * n_blk, SGU_BLOCK, D_SGU), vs.reshape(B * n_blk, SGU_BLOCK, D_SGU),
                  sgu_w_s[l], sgu_b_s[l], SGU_BLOCK)
        xp, xnp = _merge(o.reshape(B * S, D_ATTN), sg.reshape(B * S, D_SGU), gates, xp,
                         w_pa, w_pb, w_o, norm_ffn_g[l], _row_tile(B * S, 256))
        nk_p.append(kv3[:, S - WINDOW:, :D_KV].reshape(B, WINDOW, N_KV_HEADS, HEAD_DIM))
        nv_p.append(kv3[:, S - WINDOW:, D_KV:].reshape(B, WINDOW, N_KV_HEADS, HEAD_DIM))

        qs, kvs, us, vss, gates_s = _in_projection(xs, norm_mix_g[l], w_in_l, sgu_norm_g[l])
        kvs3 = kvs.reshape(Bd, T, 2 * D_KV)
        k_all = jnp.concatenate([cache_k_swa[l].reshape(Bd, WINDOW, D_KV).astype(BF16),
                                 kvs3[..., :D_KV].astype(BF16)], axis=1)
        v_all = jnp.concatenate([cache_v_swa[l].reshape(Bd, WINDOW, D_KV).astype(BF16),
                                 kvs3[..., D_KV:].astype(BF16)], axis=1)
        os_ = _attention(qs.reshape(Bd, T, D_ATTN), k_all, v_all, bias_s, attn_sinks[l],
                         T, WINDOW + T, 0)
        ss = _sgu(us.reshape(Bd, T, D_SGU), vss.reshape(Bd, T, D_SGU), sgu_w_s[l], sgu_b_s[l], T)
        xs, xns = _merge(os_.reshape(Bd * T, D_ATTN), ss.reshape(Bd * T, D_SGU), gates_s, xs,
                         w_pa, w_pb, w_o, norm_ffn_g[l], Bd * T)
        nk_s.append(kvs3[..., :D_KV].reshape(Bd, T, N_KV_HEADS, HEAD_DIM))
        nv_s.append(kvs3[..., D_KV:].reshape(Bd, T, N_KV_HEADS, HEAD_DIM))
        nsgu_s.append(vss.reshape(Bd, T, D_SGU))

        xp = _peer(xnp, xp, w_q, keys, e_u, e_vt, g_out, last)
        xs = _peer(xns, xs, w_q, keys, e_u, e_vt, g_out, last)

    return (xp.reshape(B, S, D), xs.reshape(Bd, T, D), jnp.stack(nk_p), jnp.stack(nv_p),
            jnp.stack(nk_s), jnp.stack(nv_s), jnp.stack(nsgu_s))
```

```python
import functools
import math

import jax
import jax.numpy as jnp
import numpy as np
from jax import lax
from jax.experimental import pallas as pl
from jax.experimental.pallas import tpu as pltpu

F32 = jnp.float32
BF16 = jnp.bfloat16

CHUNK = 64
WINDOW = 128
N_HEADS = 16
N_KV_HEADS = 2
HEAD_DIM = 64
GQA_GROUP = N_HEADS // N_KV_HEADS
D_ATTN = N_HEADS * HEAD_DIM
D_KV = N_KV_HEADS * HEAD_DIM
N_BUCKETS = 32
MAX_DISTANCE = 128
SGU_BLOCK = 128
SGU_GROUPS = 8
SGU_GROUP_DIM = 128
D_SGU = SGU_GROUPS * SGU_GROUP_DIM
PEER_HEADS = 8
PEER_N_KEYS = 128
PEER_HALF = 128
PEER_TOPK = 16
EPS = 1e-6
NEG_INF = -1e30

LANES = 128
VMEM_LIMIT_BYTES = 56 * 1024 * 1024

_CANDS = tuple((k, l) for k in range(PEER_TOPK) for l in range(PEER_TOPK)
               if (k + 1) * (l + 1) <= PEER_TOPK)


def _cparams(*sem):
    return pltpu.CompilerParams(dimension_semantics=sem, vmem_limit_bytes=VMEM_LIMIT_BYTES)


def _gelu_tanh(x):
    c = math.sqrt(2.0 / math.pi)
    t = jnp.tanh(x * (c + (c * 0.044715) * (x * x)))
    hx = 0.5 * x
    return hx + hx * t


def _rms(x, g):
    r = lax.rsqrt(jnp.mean(x * x, axis=-1, keepdims=True) + EPS)
    return x * r * g


def _norm_q_kernel(x_ref, g_ref, w_ref, xn_ref, q_ref):
    xn = _rms(x_ref[...], g_ref[...]).astype(xn_ref.dtype)
    xn_ref[...] = xn
    q = jnp.dot(xn, w_ref[...], preferred_element_type=F32)
    q_ref[...] = (q * (HEAD_DIM ** -0.5)).astype(q_ref.dtype)


def _norm_and_q(x, g, w_q, tm):
    t, d = x.shape
    n = w_q.shape[1]
    return pl.pallas_call(
        _norm_q_kernel,
        grid=(t // tm,),
        in_specs=[pl.BlockSpec((tm, d), lambda i: (i, 0)),
                  pl.BlockSpec((1, d), lambda i: (0, 0)),
                  pl.BlockSpec((d, n), lambda i: (0, 0), pipeline_mode=pl.Buffered(1))],
        out_specs=[pl.BlockSpec((tm, d), lambda i: (i, 0)),
                   pl.BlockSpec((tm, n), lambda i: (i, 0))],
        out_shape=[jax.ShapeDtypeStruct((t, d), BF16),
                   jax.ShapeDtypeStruct((t, n), BF16)],
        compiler_params=_cparams("parallel"),
        name="norm_proj_q",
    )(x, g.reshape(1, d), w_q)


def _mm_kernel(x_ref, w_ref, *rest, epilogue):
    acc = jnp.dot(x_ref[...], w_ref[...], preferred_element_type=F32)
    if epilogue == "none":
        (o_ref,) = rest
        o_ref[...] = acc.astype(o_ref.dtype)
    elif epilogue == "gelu":
        (o_ref,) = rest
        o_ref[...] = jax.nn.gelu(acc).astype(o_ref.dtype)
    elif epilogue == "gelu_norm":
        g_ref, o_ref = rest
        o_ref[...] = _rms(jax.nn.gelu(acc), g_ref[...]).astype(o_ref.dtype)
    elif epilogue == "sigmoid":
        (o_ref,) = rest
        o_ref[...] = jax.nn.sigmoid(acc).astype(o_ref.dtype)
    else:
        raise ValueError(epilogue)


def _mm(x, w, epilogue, out_dtype, tm, tn, g=None, name="mm"):
    t, k = x.shape
    n = w.shape[1]
    in_specs = [pl.BlockSpec((tm, k), lambda i, j: (i, 0)),
                pl.BlockSpec((k, tn), lambda i, j: (0, j))]
    args = [x, w]
    if g is not None:
        in_specs.append(pl.BlockSpec((1, tn), lambda i, j: (0, j)))
        args.append(g.reshape(1, n))
    return pl.pallas_call(
        functools.partial(_mm_kernel, epilogue=epilogue),
        grid=(t // tm, n // tn),
        in_specs=in_specs,
        out_specs=pl.BlockSpec((tm, tn), lambda i, j: (i, j)),
        out_shape=jax.ShapeDtypeStruct((t, n), out_dtype),
        compiler_params=_cparams("parallel", "parallel"),
        name=name,
    )(*args)


def _bias_kernel(table_ref, bucket_ref, o_ref):
    bucket = bucket_ref[...]
    for h in range(N_HEADS):
        acc = jnp.zeros(bucket.shape, F32)
        for b in range(N_BUCKETS):
            acc = jnp.where(bucket == b, table_ref[b, h], acc)
        o_ref[h] = acc


def _t5_bucket(rel):
    half = N_BUCKETS // 2
    max_exact = half // 2
    ret = jnp.where(rel > 0, half, 0)
    n = jnp.abs(rel)
    nf = jnp.maximum(n, 1).astype(jnp.float32)
    large = max_exact + (jnp.log(nf / max_exact) / math.log(MAX_DISTANCE / max_exact)
                         * (half - max_exact)).astype(jnp.int32)
    large = jnp.minimum(large, half - 1)
    return ret + jnp.where(n < max_exact, n, large)


def _rel_bias(table, n_q, n_k):
    rel = (jnp.arange(n_k, dtype=jnp.int32)[None, :] - WINDOW
           - jnp.arange(n_q, dtype=jnp.int32)[:, None])
    bucket = _t5_bucket(rel).astype(jnp.int32) & (N_BUCKETS - 1)
    return pl.pallas_call(
        _bias_kernel,
        in_specs=[pl.BlockSpec(memory_space=pltpu.SMEM),
                  pl.BlockSpec(memory_space=pltpu.VMEM)],
        out_specs=pl.BlockSpec(memory_space=pltpu.VMEM),
        out_shape=jax.ShapeDtypeStruct((N_HEADS, n_q, n_k), F32),
        name="rel_bias",
    )(table.astype(F32), bucket)


def _attn_kernel(sinks_ref, q_ref, k_ref, v_ref, bias_ref, o_ref, *, n_q, n_k, n_pad, n_sub):
    heads = range(N_HEADS)
    kv_cols = [slice((h // GQA_GROUP) * HEAD_DIM, (h // GQA_GROUP + 1) * HEAD_DIM) for h in heads]
    for sub in range(n_sub):
        c = pl.program_id(1) * n_sub + sub
        start = pl.multiple_of(c * n_q, n_q)
        q_rows = slice(sub * n_q, (sub + 1) * n_q)
        kb = k_ref[0, pl.ds(start, n_k), :]
        vb = v_ref[0, pl.ds(start, n_k), :]
        key_row = start + lax.broadcasted_iota(jnp.int32, (1, n_k), 1)
        valid = key_row >= n_pad
        s = [lax.dot_general(q_ref[0, q_rows, h * HEAD_DIM:(h + 1) * HEAD_DIM], kb[:, kv_cols[h]],
                             (((1,), (1,)), ((), ())), preferred_element_type=F32) for h in heads]
        s = [jnp.where(valid, s[h] + bias_ref[h], NEG_INF) for h in heads]
        m = [jnp.maximum(jnp.max(s[h], axis=-1, keepdims=True), sinks_ref[h]) for h in heads]
        p = [jnp.exp(s[h] - m[h]) for h in heads]
        denom = [jnp.sum(p[h], axis=-1, keepdims=True) + jnp.exp(sinks_ref[h] - m[h])
                 for h in heads]
        o = [jnp.dot(p[h].astype(BF16), vb[:, kv_cols[h]], preferred_element_type=F32)
             for h in heads]
        for h in heads:
            o_ref[0, q_rows, h * HEAD_DIM:(h + 1) * HEAD_DIM] = (o[h] / denom[h]).astype(o_ref.dtype)


def _attention(q, k, v, bias, sinks, n_q, n_k, n_pad):
    nb, sq, _ = q.shape
    rows = k.shape[1]
    n_sub = 2 if (sq // n_q) % 2 == 0 else 1
    return pl.pallas_call(
        functools.partial(_attn_kernel, n_q=n_q, n_k=n_k, n_pad=n_pad, n_sub=n_sub),
        grid=(nb, sq // (n_q * n_sub)),
        in_specs=[pl.BlockSpec(memory_space=pltpu.SMEM),
                  pl.BlockSpec((1, n_q * n_sub, D_ATTN), lambda b, c: (b, c, 0)),
                  pl.BlockSpec((1, rows, D_KV), lambda b, c: (b, 0, 0)),
                  pl.BlockSpec((1, rows, D_KV), lambda b, c: (b, 0, 0)),
                  pl.BlockSpec((N_HEADS, n_q, n_k), lambda b, c: (0, 0, 0))],
        out_specs=pl.BlockSpec((1, n_q * n_sub, D_ATTN), lambda b, c: (b, c, 0)),
        out_shape=jax.ShapeDtypeStruct((nb, sq, D_ATTN), BF16),
        compiler_params=_cparams("parallel", "arbitrary"),
        name="swa_attention",
    )(sinks.astype(F32), q, k, v, bias)


def _sgu_kernel(u_ref, v_ref, w_ref, b_ref, o_ref, *, rows):
    r = lax.broadcasted_iota(jnp.int32, (rows, rows), 0)
    c = lax.broadcasted_iota(jnp.int32, (rows, rows), 1)
    tril = c <= r
    for g in range(SGU_GROUPS):
        sl = slice(g * SGU_GROUP_DIM, (g + 1) * SGU_GROUP_DIM)
        w = jnp.where(tril, w_ref[g], 0.0).astype(BF16)
        mixed = jnp.dot(w, v_ref[0, :, sl].astype(BF16), preferred_element_type=F32)
        o_ref[0, :, sl] = (u_ref[0, :, sl].astype(F32) * (mixed + b_ref[:, sl])).astype(o_ref.dtype)


def _sgu(u, vn, w_s, b_s, rows):
    nblk = u.shape[0]
    w = w_s[:, :rows, :rows]
    b_full = jnp.repeat(b_s[:, :rows].T, SGU_GROUP_DIM, axis=1).astype(F32)
    return pl.pallas_call(
        functools.partial(_sgu_kernel, rows=rows),
        grid=(nblk,),
        in_specs=[pl.BlockSpec((1, rows, D_SGU), lambda i: (i, 0, 0)),
                  pl.BlockSpec((1, rows, D_SGU), lambda i: (i, 0, 0)),
                  pl.BlockSpec((SGU_GROUPS, rows, rows), lambda i: (0, 0, 0)),
                  pl.BlockSpec((rows, D_SGU), lambda i: (0, 0))],
        out_specs=pl.BlockSpec((1, rows, D_SGU), lambda i: (i, 0, 0)),
        out_shape=jax.ShapeDtypeStruct((nblk, rows, D_SGU), BF16),
        compiler_params=_cparams("parallel"),
        name="sgu",
    )(u, vn, w, b_full)


def _merge_kernel(o_ref, s_ref, gate_ref, x_ref, wpa_ref, wpb_ref, wout_ref, gn_ref,
                  xo_ref, xn_ref):
    d = x_ref.shape[-1]
    ta = jnp.dot(o_ref[...], wpa_ref[...], preferred_element_type=F32)
    tb = jnp.dot(s_ref[...], wpb_ref[...], preferred_element_type=F32)
    h = gate_ref[:, :d].astype(F32) * ta + gate_ref[:, d:].astype(F32) * tb
    y = x_ref[...] + jnp.dot(h.astype(BF16), wout_ref[...], preferred_element_type=F32)
    xo_ref[...] = y
    xn_ref[...] = _rms(y, gn_ref[...]).astype(xn_ref.dtype)


def _merge(o, s, gates, x, w_pa, w_pb, w_out, g_ffn, tm):
    t, d = x.shape
    once = pl.Buffered(1)
    return pl.pallas_call(
        _merge_kernel,
        grid=(t // tm,),
        in_specs=[pl.BlockSpec((tm, D_ATTN), lambda i: (i, 0)),
                  pl.BlockSpec((tm, D_SGU), lambda i: (i, 0)),
                  pl.BlockSpec((tm, 2 * d), lambda i: (i, 0)),
                  pl.BlockSpec((tm, d), lambda i: (i, 0)),
                  pl.BlockSpec((D_ATTN, d), lambda i: (0, 0), pipeline_mode=once),
                  pl.BlockSpec((D_SGU, d), lambda i: (0, 0), pipeline_mode=once),
                  pl.BlockSpec((d, d), lambda i: (0, 0), pipeline_mode=once),
                  pl.BlockSpec((1, d), lambda i: (0, 0))],
        out_specs=[pl.BlockSpec((tm, d), lambda i: (i, 0)),
                   pl.BlockSpec((tm, d), lambda i: (i, 0))],
        out_shape=[jax.ShapeDtypeStruct((t, d), F32),
                   jax.ShapeDtypeStruct((t, d), BF16)],
        compiler_params=_cparams("parallel"),
        name="merge",
    )(o, s, gates, x, w_pa, w_pb, w_out, g_ffn.reshape(1, d))


def _peer_topk_kernel(q_ref, keys_ref, rankb_out_ref, e1_ref, nsel_ref, coef_ref,
                      s_ref, ranka_ref, rankb_ref, top0_ref, top1_ref, cand_ref, cnt_ref, zinv_ref):
    nk = PEER_N_KEYS
    tb = q_ref.shape[0]
    q = q_ref[...]
    for hp in range(2 * PEER_HEADS):
        s_ref[hp * nk:(hp + 1) * nk, :] = lax.dot_general(
            keys_ref[hp], q[:, hp * PEER_HALF:(hp + 1) * PEER_HALF],
            (((1,), (1,)), ((), ())), preferred_element_type=F32)

    key_id = lax.broadcasted_iota(jnp.int32, (nk, tb), 0).astype(F32)
    heads_per_loop = 2
    for h0 in range(0, PEER_HEADS, heads_per_loop):
        heads = range(h0, h0 + heads_per_loop)
        for h in heads:
            rows = slice(h * nk, (h + 1) * nk)
            ranka_ref[rows, :] = jnp.full((nk, tb), float(PEER_TOPK), F32)
            rankb_ref[rows, :] = jnp.full((nk, tb), float(PEER_TOPK), F32)

        def body(r, works, heads=heads):
            rf = jnp.asarray(r, F32)
            out = []
            chains = [(h, rank_ref, top_ref) for h in heads
                      for rank_ref, top_ref in ((ranka_ref, top0_ref), (rankb_ref, top1_ref))]
            for w, (h, rank_ref, top_ref) in zip(works, chains):
                rows = slice(h * nk, (h + 1) * nk)
                m = jnp.max(w, axis=0, keepdims=True)
                first = jnp.min(jnp.where(w == m, key_id, float(nk)), axis=0, keepdims=True)
                hit = key_id == first
                out.append(jnp.where(hit, -jnp.inf, w))
                rank_ref[rows, :] = jnp.where(hit, rf, rank_ref[rows, :])
                top_ref[r, pl.ds(h, 1), :] = m
            return tuple(out)

        lax.fori_loop(0, PEER_TOPK, body,
                      tuple(s_ref[hp * nk:(hp + 1) * nk, :]
                            for hp in range(2 * h0, 2 * (h0 + heads_per_loop))))

    for c, (k, l) in enumerate(_CANDS):
        cand_ref[c] = top0_ref[k] + top1_ref[l]
    cand_at = {kl: c for c, kl in enumerate(_CANDS)}
    row_len = [sum(1 for (k2, _) in _CANDS if k2 == k) for k in range(PEER_TOPK)]
    cv0 = cand_ref[cand_at[(0, 0)]]
    zeros = jnp.zeros(cv0.shape, F32)

    def body(r, carry):
        front, taken, z = carry
        m = functools.reduce(jnp.maximum, front)
        first = functools.reduce(
            jnp.minimum, [jnp.where(front[k] == m, float(k), float(PEER_TOPK))
                          for k in range(PEER_TOPK)])
        new_front, new_taken = [], []
        for k in range(PEER_TOPK):
            hit = first == float(k)
            n_k = taken[k] + jnp.where(hit, 1.0, 0.0)
            nxt = jnp.full(cv0.shape, -jnp.inf, F32)
            for l in range(1, row_len[k]):
                nxt = jnp.where(n_k == float(l), cand_ref[cand_at[(k, l)]], nxt)
            new_front.append(jnp.where(hit, nxt, front[k]))
            new_taken.append(n_k)
        return tuple(new_front), tuple(new_taken), z + jnp.exp(m - cv0)

    front0 = tuple(cand_ref[cand_at[(k, 0)]] for k in range(PEER_TOPK))
    _, taken, z = lax.fori_loop(0, PEER_TOPK, body, (front0, (zeros,) * PEER_TOPK, zeros))
    for k in range(PEER_TOPK):
        cnt_ref[k] = taken[k]
    zinv_ref[...] = 1.0 / z
    for h in range(PEER_HEADS):
        rows = slice(h * nk, (h + 1) * nk)
        head = pl.ds(h, 1)
        ra = ranka_ref[rows, :]
        nsel = jnp.zeros((nk, tb), F32)
        for k in range(PEER_TOPK):
            nsel = jnp.where(ra == float(k), cnt_ref[k, head, :], nsel)
        nsel_ref[h] = nsel
        s0 = s_ref[2 * h * nk:(2 * h + 1) * nk, :]
        s1 = s_ref[(2 * h + 1) * nk:(2 * h + 2) * nk, :]
        coef_ref[h] = jnp.exp(s0 - top0_ref[0, head, :]) * zinv_ref[head, :]
        e1_ref[rows, :] = jnp.exp(s1 - top1_ref[0, head, :]).astype(e1_ref.dtype)
        rankb_out_ref[rows, :] = rankb_ref[rows, :].astype(rankb_out_ref.dtype)


def _peer_topk(q, keys, tb):
    t = q.shape[0]
    rows = PEER_HEADS * PEER_N_KEYS
    out_w = jax.ShapeDtypeStruct((PEER_HEADS, PEER_N_KEYS, t), F32)
    out_b = jax.ShapeDtypeStruct((rows, t), BF16)
    ospec = pl.BlockSpec((rows, tb), lambda i: (0, i))
    wspec = pl.BlockSpec((PEER_HEADS, PEER_N_KEYS, tb), lambda i: (0, 0, i))
    hv = (PEER_HEADS, tb)
    return pl.pallas_call(
        _peer_topk_kernel,
        grid=(t // tb,),
        in_specs=[pl.BlockSpec((tb, q.shape[1]), lambda i: (i, 0)),
                  pl.BlockSpec(keys.shape, lambda i: (0, 0, 0))],
        out_specs=[ospec, ospec, wspec, wspec],
        out_shape=[out_b, out_b, out_w, out_w],
        scratch_shapes=[pltpu.VMEM((2 * rows, tb), F32),
                        pltpu.VMEM((rows, tb), F32),
                        pltpu.VMEM((rows, tb), F32),
                        pltpu.VMEM((PEER_TOPK,) + hv, F32),
                        pltpu.VMEM((PEER_TOPK,) + hv, F32),
                        pltpu.VMEM((len(_CANDS),) + hv, F32),
                        pltpu.VMEM((PEER_TOPK,) + hv, F32),
                        pltpu.VMEM(hv, F32)],
        compiler_params=_cparams("parallel"),
        name="peer_topk",
    )(q, keys)


def _peer_expert_kernel(xn_ref, u_ref, vt_ref, rankb_ref, e1_ref, nsel_ref, coef_ref,
                        xres_ref, gf_ref, y_ref, acc_ref, *, n_chunks, i_per_chunk, ahead,
                        final_norm):
    et = pl.program_id(1)
    nk = PEER_N_KEYS
    ck = i_per_chunk * nk

    @pl.when(et == 0)
    def _():
        acc_ref[...] = jnp.zeros_like(acc_ref)

    def rows_of(row_ref, h, il):
        row = row_ref[h, pl.ds(il, 1), :].astype(BF16)
        return jnp.broadcast_to(row, (nk, row.shape[-1]))

    def up_proj(c):
        return lax.dot_general(u_ref[c * ck:(c + 1) * ck, :], xn_ref[...], (((1,), (1,)), ((), ())),
                               preferred_element_type=F32)

    a_next = [up_proj(c) for c in range(min(ahead, n_chunks))]
    zero = jnp.zeros((), BF16)
    for c in range(n_chunks):
        if c + ahead < n_chunks:
            a_next.append(up_proj(c + ahead))
        a_t = a_next[c]
        hidden = []
        for ic in range(i_per_chunk):
            il = c * i_per_chunk + ic
            gate = None
            for h in range(PEER_HEADS):
                hs = slice(h * nk, (h + 1) * nk)
                sel = jnp.where(rankb_ref[hs, :] < rows_of(nsel_ref, h, il), e1_ref[hs, :], zero)
                sel = sel * rows_of(coef_ref, h, il)
                gate = sel if gate is None else gate + sel
            act = _gelu_tanh(a_t[ic * nk:(ic + 1) * nk, :]).astype(BF16)
            hidden.append(gate * act)
        acc_ref[...] += jnp.dot(vt_ref[:, c * ck:(c + 1) * ck], jnp.concatenate(hidden, axis=0),
                                preferred_element_type=F32)

    @pl.when(et == pl.num_programs(1) - 1)
    def _():
        y = xres_ref[...] + acc_ref[...].T
        if final_norm:
            y = _rms(y, gf_ref[...])
        y_ref[...] = y


def _peer_experts(xn, u, v_t, rankb, e1, nsel, coef, xres, g_final, tb, e_tile, i_per_chunk,
                  ahead, final_norm):
    t, d = xn.shape
    rows = PEER_HEADS * PEER_N_KEYS
    aux = pl.BlockSpec((rows, tb), lambda i, j: (0, i))
    per_tile = pl.BlockSpec((PEER_HEADS, e_tile // PEER_N_KEYS, tb), lambda i, j: (0, j, i))
    return pl.pallas_call(
        functools.partial(_peer_expert_kernel, n_chunks=e_tile // (i_per_chunk * PEER_N_KEYS),
                          i_per_chunk=i_per_chunk, ahead=ahead, final_norm=final_norm),
        grid=(t // tb, u.shape[0] // e_tile),
        in_specs=[pl.BlockSpec((tb, d), lambda i, j: (i, 0)),
                  pl.BlockSpec((e_tile, d), lambda i, j: (j, 0)),
                  pl.BlockSpec((d, e_tile), lambda i, j: (0, j)),
                  aux, aux, per_tile, per_tile,
                  pl.BlockSpec((tb, d), lambda i, j: (i, 0)),
                  pl.BlockSpec((1, d), lambda i, j: (0, 0))],
        out_specs=pl.BlockSpec((tb, d), lambda i, j: (i, 0)),
        out_shape=jax.ShapeDtypeStruct((t, d), F32),
        scratch_shapes=[pltpu.VMEM((d, tb), F32)],
        compiler_params=_cparams("parallel", "arbitrary"),
        name="peer_experts",
    )(xn, u, v_t, rankb, e1, nsel, coef, xres, g_final.reshape(1, d))


MM_ROW_TILE = 1024
MM_COL_TILE = 1024
NORM_ROW_TILE = 512
MERGE_ROW_TILE = 256
TOPK_TOKEN_TILE = LANES
PEER_TOKEN_TILE = 512
PEER_EXPERT_TILE = 1024
PEER_KEYS_PER_CHUNK = 2
PEER_UP_PROJ_AHEAD = 2


def _row_tile(t, pref):
    return pref if t % pref == 0 else t


def _in_projection(x2, g_mix, w_in, sgu_g):
    t = x2.shape[0]
    tm = _row_tile(t, MM_ROW_TILE)
    c0, c1, c2, c3, c4 = D_ATTN, D_ATTN + 2 * D_KV, D_ATTN + 2 * D_KV + D_SGU, \
        D_ATTN + 2 * D_KV + 2 * D_SGU, w_in.shape[1]
    xn, q = _norm_and_q(x2, g_mix, w_in[:, :c0], _row_tile(t, NORM_ROW_TILE))
    kv = _mm(xn, w_in[:, c0:c1], "none", F32, tm, 2 * D_KV, name="proj_kv")
    u = _mm(xn, w_in[:, c1:c2], "gelu", BF16, tm, MM_COL_TILE, name="proj_u")
    vs = _mm(xn, w_in[:, c2:c3], "gelu_norm", F32, _row_tile(t, NORM_ROW_TILE), D_SGU, g=sgu_g,
             name="proj_vsgu")
    gates = _mm(xn, w_in[:, c3:c4], "sigmoid", BF16, tm, MM_COL_TILE, name="proj_gates")
    return q, kv, u, vs, gates


def _peer(xn, xres, w_query, keys, u, v_t, g_final, final_norm):
    t = xn.shape[0]
    q = _mm(xn, w_query, "none", BF16, _row_tile(t, MM_ROW_TILE), MM_COL_TILE, name="peer_query")
    rankb, e1, nsel, coef = _peer_topk(q, keys, TOPK_TOKEN_TILE)
    return _peer_experts(xn, u, v_t, rankb, e1, nsel, coef, xres, g_final,
                         _row_tile(t, PEER_TOKEN_TILE), PEER_EXPERT_TILE, PEER_KEYS_PER_CHUNK,
                         PEER_UP_PROJ_AHEAD, final_norm)


def kernel(x_prompt, x_sample, cache_k_swa, cache_v_swa, norm_mix_g, w_in, sgu_norm_g, sgu_w_s,
           sgu_b_s, attn_sinks, rel_bias_table, w_branch_attn, w_branch_sgu, w_out, norm_ffn_g,
           peer_w_query, peer_sub_keys, peer_expert_u, peer_expert_v, norm_final_g):
    B, S, D = x_prompt.shape
    Bd, T, _ = x_sample.shape
    depth = w_in.shape[0]
    bias_p = _rel_bias(rel_bias_table, CHUNK, WINDOW + CHUNK)
    bias_s = _rel_bias(rel_bias_table, T, WINDOW + T)

    xp = x_prompt.reshape(B * S, D)
    xs = x_sample.reshape(Bd * T, D)
    nk_p, nv_p, nk_s, nv_s, nsgu_s = [], [], [], [], []
    for l in range(depth):
        last = l == depth - 1
        w_in_l = w_in[l].astype(BF16)
        w_pa = w_branch_attn[l].astype(BF16)
        w_pb = w_branch_sgu[l].astype(BF16)
        w_o = w_out[l].astype(BF16)
        w_q = peer_w_query[l].astype(BF16)
        keys = peer_sub_keys[l].astype(BF16).reshape(2 * PEER_HEADS, PEER_N_KEYS, PEER_HALF)
        e_u = peer_expert_u[l].astype(BF16)
        e_vt = peer_expert_v[l].T.astype(BF16)
        g_out = norm_final_g if last else jnp.ones_like(norm_final_g)

        q, kv, u, vs, gates = _in_projection(xp, norm_mix_g[l], w_in_l, sgu_norm_g[l])
        kv3 = kv.reshape(B, S, 2 * D_KV)
        kpad = jnp.pad(kv3[..., :D_KV].astype(BF16), ((0, 0), (WINDOW, 0), (0, 0)))
        vpad = jnp.pad(kv3[..., D_KV:].astype(BF16), ((0, 0), (WINDOW, 0), (0, 0)))
        o = _attention(q.reshape(B, S, D_ATTN), kpad, vpad, bias_p, attn_sinks[l],
                       CHUNK, WINDOW + CHUNK, WINDOW)
        n_blk = S // SGU_BLOCK
        sg = _sgu(u.reshape(B * n_blk, SGU_BLOCK, D_SGU), vs.reshape(B * n_blk, SGU_BLOCK, D_SGU),
                  sgu_w_s[l], sgu_b_s[l], SGU_BLOCK)
        xp, xnp = _merge(o.reshape(B * S, D_ATTN), sg.reshape(B * S, D_SGU), gates, xp,
                         w_pa, w_pb, w_o, norm_ffn_g[l], _row_tile(B * S, MERGE_ROW_TILE))
        nk_p.append(kv3[:, S - WINDOW:, :D_KV].reshape(B, WINDOW, N_KV_HEADS, HEAD_DIM))
        nv_p.append(kv3[:, S - WINDOW:, D_KV:].reshape(B, WINDOW, N_KV_HEADS, HEAD_DIM))

        qs, kvs, us, vss, gates_s = _in_projection(xs, norm_mix_g[l], w_in_l, sgu_norm_g[l])
        kvs3 = kvs.reshape(Bd, T, 2 * D_KV)
        k_all = jnp.concatenate([cache_k_swa[l].reshape(Bd, WINDOW, D_KV).astype(BF16),
                                 kvs3[..., :D_KV].astype(BF16)], axis=1)
        v_all = jnp.concatenate([cache_v_swa[l].reshape(Bd, WINDOW, D_KV).astype(BF16),
                                 kvs3[..., D_KV:].astype(BF16)], axis=1)
        os_ = _attention(qs.reshape(Bd, T, D_ATTN), k_all, v_all, bias_s, attn_sinks[l],
                         T, WINDOW + T, 0)
        ss = _sgu(us.reshape(Bd, T, D_SGU), vss.reshape(Bd, T, D_SGU), sgu_w_s[l], sgu_b_s[l], T)
        xs, xns = _merge(os_.reshape(Bd * T, D_ATTN), ss.reshape(Bd * T, D_SGU), gates_s, xs,
                         w_pa, w_pb, w_o, norm_ffn_g[l], Bd * T)
        nk_s.append(kvs3[..., :D_KV].reshape(Bd, T, N_KV_HEADS, HEAD_DIM))
        nv_s.append(kvs3[..., D_KV:].reshape(Bd, T, N_KV_HEADS, HEAD_DIM))
        nsgu_s.append(vss.reshape(Bd, T, D_SGU))

        xp = _peer(xnp, xp, w_q, keys, e_u, e_vt, g_out, last)
        xs = _peer(xns, xs, w_q, keys, e_u, e_vt, g_out, last)

    return (xp.reshape(B, S, D), xs.reshape(Bd, T, D), jnp.stack(nk_p), jnp.stack(nv_p),
            jnp.stack(nk_s), jnp.stack(nv_s), jnp.stack(nsgu_s))
```

```python
import functools
import math

import jax
import jax.numpy as jnp
import numpy as np
from jax import lax
from jax.experimental import pallas as pl
from jax.experimental.pallas import tpu as pltpu

F32 = jnp.float32
BF16 = jnp.bfloat16

CHUNK = 64
WINDOW = 128
N_HEADS = 16
N_KV_HEADS = 2
HEAD_DIM = 64
GQA_GROUP = N_HEADS // N_KV_HEADS
D_ATTN = N_HEADS * HEAD_DIM
D_KV = N_KV_HEADS * HEAD_DIM
N_BUCKETS = 32
MAX_DISTANCE = 128
SGU_BLOCK = 128
SGU_GROUPS = 8
SGU_GROUP_DIM = 128
D_SGU = SGU_GROUPS * SGU_GROUP_DIM
PEER_HEADS = 8
PEER_N_KEYS = 128
PEER_HALF = 128
PEER_TOPK = 16
EPS = 1e-6
NEG_INF = -1e30

LANES = 128
VMEM_LIMIT_BYTES = 56 * 1024 * 1024

_CANDS = tuple((k, l) for k in range(PEER_TOPK) for l in range(PEER_TOPK)
               if (k + 1) * (l + 1) <= PEER_TOPK)


def _cparams(*sem):
    return pltpu.CompilerParams(dimension_semantics=sem, vmem_limit_bytes=VMEM_LIMIT_BYTES)


def _gelu_tanh(x):
    c = math.sqrt(2.0 / math.pi)
    t = jnp.tanh(x * (c + (c * 0.044715) * (x * x)))
    hx = 0.5 * x
    return hx + hx * t


def _rms(x, g):
    r = lax.rsqrt(jnp.mean(x * x, axis=-1, keepdims=True) + EPS)
    return x * r * g


def _norm_q_kernel(x_ref, g_ref, w_ref, xn_ref, q_ref):
    xn = _rms(x_ref[...], g_ref[...]).astype(xn_ref.dtype)
    xn_ref[...] = xn
    q = jnp.dot(xn, w_ref[...], preferred_element_type=F32)
    q_ref[...] = (q * (HEAD_DIM ** -0.5)).astype(q_ref.dtype)


def _norm_and_q(x, g, w_q, tm):
    t, d = x.shape
    n = w_q.shape[1]
    return pl.pallas_call(
        _norm_q_kernel,
        grid=(t // tm,),
        in_specs=[pl.BlockSpec((tm, d), lambda i: (i, 0)),
                  pl.BlockSpec((1, d), lambda i: (0, 0)),
                  pl.BlockSpec((d, n), lambda i: (0, 0), pipeline_mode=pl.Buffered(1))],
        out_specs=[pl.BlockSpec((tm, d), lambda i: (i, 0)),
                   pl.BlockSpec((tm, n), lambda i: (i, 0))],
        out_shape=[jax.ShapeDtypeStruct((t, d), BF16),
                   jax.ShapeDtypeStruct((t, n), BF16)],
        compiler_params=_cparams("parallel"),
        name="norm_proj_q",
    )(x, g.reshape(1, d), w_q)


def _mm_kernel(x_ref, w_ref, *rest, epilogue):
    acc = jnp.dot(x_ref[...], w_ref[...], preferred_element_type=F32)
    if epilogue == "none":
        (o_ref,) = rest
        o_ref[...] = acc.astype(o_ref.dtype)
    elif epilogue == "gelu":
        (o_ref,) = rest
        o_ref[...] = jax.nn.gelu(acc).astype(o_ref.dtype)
    elif epilogue == "gelu_norm":
        g_ref, o_ref = rest
        o_ref[...] = _rms(jax.nn.gelu(acc), g_ref[...]).astype(o_ref.dtype)
    elif epilogue == "sigmoid":
        (o_ref,) = rest
        o_ref[...] = jax.nn.sigmoid(acc).astype(o_ref.dtype)
    else:
        raise ValueError(epilogue)


def _mm(x, w, epilogue, out_dtype, tm, tn, g=None, name="mm"):
    t, k = x.shape
    n = w.shape[1]
    in_specs = [pl.BlockSpec((tm, k), lambda i, j: (i, 0)),
                pl.BlockSpec((k, tn), lambda i, j: (0, j))]
    args = [x, w]
    if g is not None:
        in_specs.append(pl.BlockSpec((1, tn), lambda i, j: (0, j)))
        args.append(g.reshape(1, n))
    return pl.pallas_call(
        functools.partial(_mm_kernel, epilogue=epilogue),
        grid=(t // tm, n // tn),
        in_specs=in_specs,
        out_specs=pl.BlockSpec((tm, tn), lambda i, j: (i, j)),
        out_shape=jax.ShapeDtypeStruct((t, n), out_dtype),
        compiler_params=_cparams("parallel", "parallel"),
        name=name,
    )(*args)


def _bias_kernel(table_ref, bucket_ref, o_ref):
    bucket = bucket_ref[...]
    for h in range(N_HEADS):
        acc = jnp.zeros(bucket.shape, F32)
        for b in range(N_BUCKETS):
            acc = jnp.where(bucket == b, table_ref[b, h], acc)
        o_ref[h] = acc


def _t5_bucket(rel):
    half = N_BUCKETS // 2
    max_exact = half // 2
    ret = jnp.where(rel > 0, half, 0)
    n = jnp.abs(rel)
    nf = jnp.maximum(n, 1).astype(jnp.float32)
    large = max_exact + (jnp.log(nf / max_exact) / math.log(MAX_DISTANCE / max_exact)
                         * (half - max_exact)).astype(jnp.int32)
    large = jnp.minimum(large, half - 1)
    return ret + jnp.where(n < max_exact, n, large)


def _rel_bias(table, n_q, n_k):
    rel = (jnp.arange(n_k, dtype=jnp.int32)[None, :] - WINDOW
           - jnp.arange(n_q, dtype=jnp.int32)[:, None])
    bucket = _t5_bucket(rel).astype(jnp.int32) & (N_BUCKETS - 1)
    return pl.pallas_call(
        _bias_kernel,
        in_specs=[pl.BlockSpec(memory_space=pltpu.SMEM),
                  pl.BlockSpec(memory_space=pltpu.VMEM)],
        out_specs=pl.BlockSpec(memory_space=pltpu.VMEM),
        out_shape=jax.ShapeDtypeStruct((N_HEADS, n_q, n_k), F32),
        name="rel_bias",
    )(table.astype(F32), bucket)


def _attn_kernel(sinks_ref, q_ref, k_ref, v_ref, bias_ref, o_ref, *, n_q, n_k, n_pad, n_sub):
    heads = range(N_HEADS)
    kv_cols = [slice((h // GQA_GROUP) * HEAD_DIM, (h // GQA_GROUP + 1) * HEAD_DIM) for h in heads]
    for sub in range(n_sub):
        c = pl.program_id(1) * n_sub + sub
        start = pl.multiple_of(c * n_q, n_q)
        q_rows = slice(sub * n_q, (sub + 1) * n_q)
        kb = k_ref[0, pl.ds(start, n_k), :]
        vb = v_ref[0, pl.ds(start, n_k), :]
        key_row = start + lax.broadcasted_iota(jnp.int32, (1, n_k), 1)
        valid = key_row >= n_pad
        s = [lax.dot_general(q_ref[0, q_rows, h * HEAD_DIM:(h + 1) * HEAD_DIM], kb[:, kv_cols[h]],
                             (((1,), (1,)), ((), ())), preferred_element_type=F32) for h in heads]
        s = [jnp.where(valid, s[h] + bias_ref[h], NEG_INF) for h in heads]
        m = [jnp.maximum(jnp.max(s[h], axis=-1, keepdims=True), sinks_ref[h]) for h in heads]
        p = [jnp.exp(s[h] - m[h]) for h in heads]
        denom = [jnp.sum(p[h], axis=-1, keepdims=True) + jnp.exp(sinks_ref[h] - m[h])
                 for h in heads]
        o = [jnp.dot(p[h].astype(BF16), vb[:, kv_cols[h]], preferred_element_type=F32)
             for h in heads]
        for h in heads:
            o_ref[0, q_rows, h * HEAD_DIM:(h + 1) * HEAD_DIM] = (o[h] / denom[h]).astype(o_ref.dtype)


def _attention(q, k, v, bias, sinks, n_q, n_k, n_pad):
    nb, sq, _ = q.shape
    rows = k.shape[1]
    n_sub = 2 if (sq // n_q) % 2 == 0 else 1
    return pl.pallas_call(
        functools.partial(_attn_kernel, n_q=n_q, n_k=n_k, n_pad=n_pad, n_sub=n_sub),
        grid=(nb, sq // (n_q * n_sub)),
        in_specs=[pl.BlockSpec(memory_space=pltpu.SMEM),
                  pl.BlockSpec((1, n_q * n_sub, D_ATTN), lambda b, c: (b, c, 0)),
                  pl.BlockSpec((1, rows, D_KV), lambda b, c: (b, 0, 0)),
                  pl.BlockSpec((1, rows, D_KV), lambda b, c: (b, 0, 0)),
                  pl.BlockSpec((N_HEADS, n_q, n_k), lambda b, c: (0, 0, 0))],
        out_specs=pl.BlockSpec((1, n_q * n_sub, D_ATTN), lambda b, c: (b, c, 0)),
        out_shape=jax.ShapeDtypeStruct((nb, sq, D_ATTN), BF16),
        compiler_params=_cparams("parallel", "arbitrary"),
        name="swa_attention",
    )(sinks.astype(F32), q, k, v, bias)


def _sgu_kernel(u_ref, v_ref, w_ref, b_ref, o_ref, *, rows):
    r = lax.broadcasted_iota(jnp.int32, (rows, rows), 0)
    c = lax.broadcasted_iota(jnp.int32, (rows, rows), 1)
    tril = c <= r
    for g in range(SGU_GROUPS):
        sl = slice(g * SGU_GROUP_DIM, (g + 1) * SGU_GROUP_DIM)
        w = jnp.where(tril, w_ref[g], 0.0).astype(BF16)
        mixed = jnp.dot(w, v_ref[0, :, sl].astype(BF16), preferred_element_type=F32)
        o_ref[0, :, sl] = (u_ref[0, :, sl].astype(F32) * (mixed + b_ref[:, sl])).astype(o_ref.dtype)


def _sgu(u, vn, w_s, b_s, rows):
    nblk = u.shape[0]
    w = w_s[:, :rows, :rows]
    b_full = jnp.repeat(b_s[:, :rows].T, SGU_GROUP_DIM, axis=1).astype(F32)
    return pl.pallas_call(
        functools.partial(_sgu_kernel, rows=rows),
        grid=(nblk,),
        in_specs=[pl.BlockSpec((1, rows, D_SGU), lambda i: (i, 0, 0)),
                  pl.BlockSpec((1, rows, D_SGU), lambda i: (i, 0, 0)),
                  pl.BlockSpec((SGU_GROUPS, rows, rows), lambda i: (0, 0, 0)),
                  pl.BlockSpec((rows, D_SGU), lambda i: (0, 0))],
        out_specs=pl.BlockSpec((1, rows, D_SGU), lambda i: (i, 0, 0)),
        out_shape=jax.ShapeDtypeStruct((nblk, rows, D_SGU), BF16),
        compiler_params=_cparams("parallel"),
        name="sgu",
    )(u, vn, w, b_full)


def _merge_kernel(o_ref, s_ref, gate_ref, x_ref, wpa_ref, wpb_ref, wout_ref, gn_ref,
                  xo_ref, xn_ref):
    d = x_ref.shape[-1]
    ta = jnp.dot(o_ref[...], wpa_ref[...], preferred_element_type=F32)
    tb = jnp.dot(s_ref[...], wpb_ref[...], preferred_element_type=F32)
    h = gate_ref[:, :d].astype(F32) * ta + gate_ref[:, d:].astype(F32) * tb
    y = x_ref[...] + jnp.dot(h.astype(BF16), wout_ref[...], preferred_element_type=F32)
    xo_ref[...] = y
    xn_ref[...] = _rms(y, gn_ref[...]).astype(xn_ref.dtype)


def _merge(o, s, gates, x, w_pa, w_pb, w_out, g_ffn, tm):
    t, d = x.shape
    once = pl.Buffered(1)
    return pl.pallas_call(
        _merge_kernel,
        grid=(t // tm,),
        in_specs=[pl.BlockSpec((tm, D_ATTN), lambda i: (i, 0)),
                  pl.BlockSpec((tm, D_SGU), lambda i: (i, 0)),
                  pl.BlockSpec((tm, 2 * d), lambda i: (i, 0)),
                  pl.BlockSpec((tm, d), lambda i: (i, 0)),
                  pl.BlockSpec((D_ATTN, d), lambda i: (0, 0), pipeline_mode=once),
                  pl.BlockSpec((D_SGU, d), lambda i: (0, 0), pipeline_mode=once),
                  pl.BlockSpec((d, d), lambda i: (0, 0), pipeline_mode=once),
                  pl.BlockSpec((1, d), lambda i: (0, 0))],
        out_specs=[pl.BlockSpec((tm, d), lambda i: (i, 0)),
                   pl.BlockSpec((tm, d), lambda i: (i, 0))],
        out_shape=[jax.ShapeDtypeStruct((t, d), F32),
                   jax.ShapeDtypeStruct((t, d), BF16)],
        compiler_params=_cparams("parallel"),
        name="merge",
    )(o, s, gates, x, w_pa, w_pb, w_out, g_ffn.reshape(1, d))


def _peer_topk_kernel(q_ref, keys_ref, rankb_out_ref, e1_ref, nsel_ref, coef_ref,
                      s_ref, ranka_ref, rankb_ref, top0_ref, top1_ref, cand_ref, cnt_ref, zinv_ref):
    nk = PEER_N_KEYS
    tb = q_ref.shape[0]
    q = q_ref[...]
    for hp in range(2 * PEER_HEADS):
        s_ref[hp * nk:(hp + 1) * nk, :] = lax.dot_general(
            keys_ref[hp], q[:, hp * PEER_HALF:(hp + 1) * PEER_HALF],
            (((1,), (1,)), ((), ())), preferred_element_type=F32)

    key_id = lax.broadcasted_iota(jnp.int32, (nk, tb), 0).astype(F32)
    heads_per_loop = 2
    for h0 in range(0, PEER_HEADS, heads_per_loop):
        heads = range(h0, h0 + heads_per_loop)
        for h in heads:
            rows = slice(h * nk, (h + 1) * nk)
            ranka_ref[rows, :] = jnp.full((nk, tb), float(PEER_TOPK), F32)
            rankb_ref[rows, :] = jnp.full((nk, tb), float(PEER_TOPK), F32)

        def body(r, works, heads=heads):
            rf = jnp.asarray(r, F32)
            out = []
            chains = [(h, rank_ref, top_ref) for h in heads
                      for rank_ref, top_ref in ((ranka_ref, top0_ref), (rankb_ref, top1_ref))]
            for w, (h, rank_ref, top_ref) in zip(works, chains):
                rows = slice(h * nk, (h + 1) * nk)
                m = jnp.max(w, axis=0, keepdims=True)
                first = jnp.min(jnp.where(w == m, key_id, float(nk)), axis=0, keepdims=True)
                hit = key_id == first
                out.append(jnp.where(hit, -jnp.inf, w))
                rank_ref[rows, :] = jnp.where(hit, rf, rank_ref[rows, :])
                top_ref[r, pl.ds(h, 1), :] = m
            return tuple(out)

        lax.fori_loop(0, PEER_TOPK, body,
                      tuple(s_ref[hp * nk:(hp + 1) * nk, :]
                            for hp in range(2 * h0, 2 * (h0 + heads_per_loop))))

    for c, (k, l) in enumerate(_CANDS):
        cand_ref[c] = top0_ref[k] + top1_ref[l]
    cand_at = {kl: c for c, kl in enumerate(_CANDS)}
    row_len = [sum(1 for (k2, _) in _CANDS if k2 == k) for k in range(PEER_TOPK)]
    cv0 = cand_ref[cand_at[(0, 0)]]
    zeros = jnp.zeros(cv0.shape, F32)

    def body(r, carry):
        front, taken, z = carry
        m = functools.reduce(jnp.maximum, front)
        first = functools.reduce(
            jnp.minimum, [jnp.where(front[k] == m, float(k), float(PEER_TOPK))
                          for k in range(PEER_TOPK)])
        new_front, new_taken = [], []
        for k in range(PEER_TOPK):
            hit = first == float(k)
            n_k = taken[k] + jnp.where(hit, 1.0, 0.0)
            nxt = jnp.full(cv0.shape, -jnp.inf, F32)
            for l in range(1, row_len[k]):
                nxt = jnp.where(n_k == float(l), cand_ref[cand_at[(k, l)]], nxt)
            new_front.append(jnp.where(hit, nxt, front[k]))
            new_taken.append(n_k)
        return tuple(new_front), tuple(new_taken), z + jnp.exp(m - cv0)

    front0 = tuple(cand_ref[cand_at[(k, 0)]] for k in range(PEER_TOPK))
    _, taken, z = lax.fori_loop(0, PEER_TOPK, body, (front0, (zeros,) * PEER_TOPK, zeros))
    for k in range(PEER_TOPK):
        cnt_ref[k] = taken[k]
    zinv_ref[...] = 1.0 / z
    for h in range(PEER_HEADS):
        rows = slice(h * nk, (h + 1) * nk)
        head = pl.ds(h, 1)
        ra = ranka_ref[rows, :]
        nsel = jnp.zeros((nk, tb), F32)
        for k in range(PEER_TOPK):
            nsel = jnp.where(ra == float(k), cnt_ref[k, head, :], nsel)
        nsel_ref[h] = nsel
        s0 = s_ref[2 * h * nk:(2 * h + 1) * nk, :]
        s1 = s_ref[(2 * h + 1) * nk:(2 * h + 2) * nk, :]
        coef_ref[h] = jnp.exp(s0 - top0_ref[0, head, :]) * zinv_ref[head, :]
        e1_ref[rows, :] = jnp.exp(s1 - top1_ref[0, head, :]).astype(e1_ref.dtype)
        rankb_out_ref[rows, :] = rankb_ref[rows, :].astype(rankb_out_ref.dtype)


def _peer_topk(q, keys, tb):
    t = q.shape[0]
    rows = PEER_HEADS * PEER_N_KEYS
    out_w = jax.ShapeDtypeStruct((PEER_HEADS, PEER_N_KEYS, t), F32)
    out_b = jax.ShapeDtypeStruct((rows, t), BF16)
    ospec = pl.BlockSpec((rows, tb), lambda i: (0, i))
    wspec = pl.BlockSpec((PEER_HEADS, PEER_N_KEYS, tb), lambda i: (0, 0, i))
    hv = (PEER_HEADS, tb)
    return pl.pallas_call(
        _peer_topk_kernel,
        grid=(t // tb,),
        in_specs=[pl.BlockSpec((tb, q.shape[1]), lambda i: (i, 0)),
                  pl.BlockSpec(keys.shape, lambda i: (0, 0, 0))],
        out_specs=[ospec, ospec, wspec, wspec],
        out_shape=[out_b, out_b, out_w, out_w],
        scratch_shapes=[pltpu.VMEM((2 * rows, tb), F32),
                        pltpu.VMEM((rows, tb), F32),
                        pltpu.VMEM((rows, tb), F32),
                        pltpu.VMEM((PEER_TOPK,) + hv, F32),
                        pltpu.VMEM((PEER_TOPK,) + hv, F32),
                        pltpu.VMEM((len(_CANDS),) + hv, F32),
                        pltpu.VMEM((PEER_TOPK,) + hv, F32),
                        pltpu.VMEM(hv, F32)],
        compiler_params=_cparams("parallel"),
        name="peer_topk",
    )(q, keys)


def _peer_expert_kernel(xn_ref, u_ref, vt_ref, rankb_ref, e1_ref, nsel_ref, coef_ref,
                        xres_ref, gf_ref, y_ref, acc_ref, *, n_chunks, i_per_chunk, ahead,
                        final_norm):
    et = pl.program_id(1)
    nk = PEER_N_KEYS
    ck = i_per_chunk * nk

    @pl.when(et == 0)
    def _():
        acc_ref[...] = jnp.zeros_like(acc_ref)

    def rows_of(row_ref, h, il):
        row = row_ref[h, pl.ds(il, 1), :].astype(BF16)
        return jnp.broadcast_to(row, (nk, row.shape[-1]))

    def up_proj(c):
        return lax.dot_general(u_ref[c * ck:(c + 1) * ck, :], xn_ref[...], (((1,), (1,)), ((), ())),
                               preferred_element_type=F32)

    a_next = [up_proj(c) for c in range(min(ahead, n_chunks))]
    zero = jnp.zeros((), BF16)
    for c in range(n_chunks):
        if c + ahead < n_chunks:
            a_next.append(up_proj(c + ahead))
        a_t = a_next[c]
        hidden = []
        for ic in range(i_per_chunk):
            il = c * i_per_chunk + ic
            gate = None
            for h in range(PEER_HEADS):
                hs = slice(h * nk, (h + 1) * nk)
                sel = jnp.where(rankb_ref[hs, :] < rows_of(nsel_ref, h, il), e1_ref[hs, :], zero)
                sel = sel * rows_of(coef_ref, h, il)
                gate = sel if gate is None else gate + sel
            act = _gelu_tanh(a_t[ic * nk:(ic + 1) * nk, :]).astype(BF16)
            hidden.append(gate * act)
        acc_ref[...] += jnp.dot(vt_ref[:, c * ck:(c + 1) * ck], jnp.concatenate(hidden, axis=0),
                                preferred_element_type=F32)

    @pl.when(et == pl.num_programs(1) - 1)
    def _():
        y = xres_ref[...] + acc_ref[...].T
        if final_norm:
            y = _rms(y, gf_ref[...])
        y_ref[...] = y


def _peer_experts(xn, u, v_t, rankb, e1, nsel, coef, xres, g_final, tb, e_tile, i_per_chunk,
                  ahead, final_norm):
    t, d = xn.shape
    rows = PEER_HEADS * PEER_N_KEYS
    aux = pl.BlockSpec((rows, tb), lambda i, j: (0, i))
    per_tile = pl.BlockSpec((PEER_HEADS, e_tile // PEER_N_KEYS, tb), lambda i, j: (0, j, i))
    return pl.pallas_call(
        functools.partial(_peer_expert_kernel, n_chunks=e_tile // (i_per_chunk * PEER_N_KEYS),
                          i_per_chunk=i_per_chunk, ahead=ahead, final_norm=final_norm),
        grid=(t // tb, u.shape[0] // e_tile),
        in_specs=[pl.BlockSpec((tb, d), lambda i, j: (i, 0)),
                  pl.BlockSpec((e_tile, d), lambda i, j: (j, 0)),
                  pl.BlockSpec((d, e_tile), lambda i, j: (0, j)),
                  aux, aux, per_tile, per_tile,
                  pl.BlockSpec((tb, d), lambda i, j: (i, 0)),
                  pl.BlockSpec((1, d), lambda i, j: (0, 0))],
        out_specs=pl.BlockSpec((tb, d), lambda i, j: (i, 0)),
        out_shape=jax.ShapeDtypeStruct((t, d), F32),
        scratch_shapes=[pltpu.VMEM((d, tb), F32)],
        compiler_params=_cparams("parallel", "arbitrary"),
        name="peer_experts",
    )(xn, u, v_t, rankb, e1, nsel, coef, xres, g_final.reshape(1, d))


MM_ROW_TILE = 1024
MM_COL_TILE = 1024
NORM_ROW_TILE = 512
MERGE_ROW_TILE = 256
TOPK_TOKEN_TILE = LANES
PEER_TOKEN_TILE = 512
PEER_EXPERT_TILE = 1024
PEER_KEYS_PER_CHUNK = 2
PEER_UP_PROJ_AHEAD = 3


def _row_tile(t, pref):
    return pref if t % pref == 0 else t


def _in_projection(x2, g_mix, w_in, sgu_g):
    t = x2.shape[0]
    tm = _row_tile(t, MM_ROW_TILE)
    c0, c1, c2, c3, c4 = D_ATTN, D_ATTN + 2 * D_KV, D_ATTN + 2 * D_KV + D_SGU, \
        D_ATTN + 2 * D_KV + 2 * D_SGU, w_in.shape[1]
    xn, q = _norm_and_q(x2, g_mix, w_in[:, :c0], _row_tile(t, NORM_ROW_TILE))
    kv = _mm(xn, w_in[:, c0:c1], "none", F32, tm, 2 * D_KV, name="proj_kv")
    u = _mm(xn, w_in[:, c1:c2], "gelu", BF16, tm, MM_COL_TILE, name="proj_u")
    vs = _mm(xn, w_in[:, c2:c3], "gelu_norm", F32, _row_tile(t, NORM_ROW_TILE), D_SGU, g=sgu_g,
             name="proj_vsgu")
    gates = _mm(xn, w_in[:, c3:c4], "sigmoid", BF16, tm, MM_COL_TILE, name="proj_gates")
    return q, kv, u, vs, gates


def _peer(xn, xres, w_query, keys, u, v_t, g_final, final_norm):
    t = xn.shape[0]
    q = _mm(xn, w_query, "none", BF16, _row_tile(t, MM_ROW_TILE), MM_COL_TILE, name="peer_query")
    rankb, e1, nsel, coef = _peer_topk(q, keys, TOPK_TOKEN_TILE)
    return _peer_experts(xn, u, v_t, rankb, e1, nsel, coef, xres, g_final,
                         _row_tile(t, PEER_TOKEN_TILE), PEER_EXPERT_TILE, PEER_KEYS_PER_CHUNK,
                         PEER_UP_PROJ_AHEAD, final_norm)


def kernel(x_prompt, x_sample, cache_k_swa, cache_v_swa, norm_mix_g, w_in, sgu_norm_g, sgu_w_s,
           sgu_b_s, attn_sinks, rel_bias_table, w_branch_attn, w_branch_sgu, w_out, norm_ffn_g,
           peer_w_query, peer_sub_keys, peer_expert_u, peer_expert_v, norm_final_g):
    B, S, D = x_prompt.shape
    Bd, T, _ = x_sample.shape
    depth = w_in.shape[0]
    bias_p = _rel_bias(rel_bias_table, CHUNK, WINDOW + CHUNK)
    bias_s = _rel_bias(rel_bias_table, T, WINDOW + T)

    xp = x_prompt.reshape(B * S, D)
    xs = x_sample.reshape(Bd * T, D)
    nk_p, nv_p, nk_s, nv_s, nsgu_s = [], [], [], [], []
    for l in range(depth):
        last = l == depth - 1
        w_in_l = w_in[l].astype(BF16)
        w_pa = w_branch_attn[l].astype(BF16)
        w_pb = w_branch_sgu[l].astype(BF16)
        w_o = w_out[l].astype(BF16)
        w_q = peer_w_query[l].astype(BF16)
        keys = peer_sub_keys[l].astype(BF16).reshape(2 * PEER_HEADS, PEER_N_KEYS, PEER_HALF)
        e_u = peer_expert_u[l].astype(BF16)
        e_vt = peer_expert_v[l].T.astype(BF16)
        g_out = norm_final_g if last else jnp.ones_like(norm_final_g)

        q, kv, u, vs, gates = _in_projection(xp, norm_mix_g[l], w_in_l, sgu_norm_g[l])
        kv3 = kv.reshape(B, S, 2 * D_KV)
        kpad = jnp.pad(kv3[..., :D_KV].astype(BF16), ((0, 0), (WINDOW, 0), (0, 0)))
        vpad = jnp.pad(kv3[..., D_KV:].astype(BF16), ((0, 0), (WINDOW, 0), (0, 0)))
        o = _attention(q.reshape(B, S, D_ATTN), kpad, vpad, bias_p, attn_sinks[l],
                       CHUNK, WINDOW + CHUNK, WINDOW)
        n_blk = S // SGU_BLOCK
        sg = _sgu(u.reshape(B * n_blk, SGU_BLOCK, D_SGU), vs.reshape(B * n_blk, SGU_BLOCK, D_SGU),
                  sgu_w_s[l], sgu_b_s[l], SGU_BLOCK)
        xp, xnp = _merge(o.reshape(B * S, D_ATTN), sg.reshape(B * S, D_SGU), gates, xp,
                         w_pa, w_pb, w_o, norm_ffn_g[l], _row_tile(B * S, MERGE_ROW_TILE))
        nk_p.append(kv3[:, S - WINDOW:, :D_KV].reshape(B, WINDOW, N_KV_HEADS, HEAD_DIM))
        nv_p.append(kv3[:, S - WINDOW:, D_KV:].reshape(B, WINDOW, N_KV_HEADS, HEAD_DIM))

        qs, kvs, us, vss, gates_s = _in_projection(xs, norm_mix_g[l], w_in_l, sgu_norm_g[l])
        kvs3 = kvs.reshape(Bd, T, 2 * D_KV)
        k_all = jnp.concatenate([cache_k_swa[l].reshape(Bd, WINDOW, D_KV).astype(BF16),
                                 kvs3[..., :D_KV].astype(BF16)], axis=1)
        v_all = jnp.concatenate([cache_v_swa[l].reshape(Bd, WINDOW, D_KV).astype(BF16),
                                 kvs3[..., D_KV:].astype(BF16)], axis=1)
        os_ = _attention(qs.reshape(Bd, T, D_ATTN), k_all, v_all, bias_s, attn_sinks[l],
                         T, WINDOW + T, 0)
        ss = _sgu(us.reshape(Bd, T, D_SGU), vss.reshape(Bd, T, D_SGU), sgu_w_s[l], sgu_b_s[l], T)
        xs, xns = _merge(os_.reshape(Bd * T, D_ATTN), ss.reshape(Bd * T, D_SGU), gates_s, xs,
                         w_pa, w_pb, w_o, norm_ffn_g[l], Bd * T)
        nk_s.append(kvs3[..., :D_KV].reshape(Bd, T, N_KV_HEADS, HEAD_DIM))
        nv_s.append(kvs3[..., D_KV:].reshape(Bd, T, N_KV_HEADS, HEAD_DIM))
        nsgu_s.append(vss.reshape(Bd, T, D_SGU))

        xp = _peer(xnp, xp, w_q, keys, e_u, e_vt, g_out, last)
        xs = _peer(xns, xs, w_q, keys, e_u, e_vt, g_out, last)

    return (xp.reshape(B, S, D), xs.reshape(Bd, T, D), jnp.stack(nk_p), jnp.stack(nv_p),
            jnp.stack(nk_s), jnp.stack(nv_s), jnp.stack(nsgu_s))
```
